```python
import math
import jax, jax.numpy as jnp
from jax import lax
import numpy as np

D_MODEL = 1024
BATCH = 8
SEQ = 4096
DEPTH = 2

GRID_W = 64
CTX_LEN = 256
N_MIXERS = 2
N_LRU_LAYERS = (DEPTH + N_MIXERS - 1) // N_MIXERS
N_ATTN_LAYERS = DEPTH // N_MIXERS
D_RNN = D_MODEL
N_LRU_BLOCKS = 8
LRU_BLOCK = D_RNN // N_LRU_BLOCKS
CONV_W = 4
LRU_C = 8.0
HEAD_DIM = 64
N_HEADS = D_MODEL // HEAD_DIM
N_KV_HEADS = 4
KV_GROUP = N_HEADS // N_KV_HEADS
ROPE_THETA = 10000.0
Q_BLOCK = 128
N_EXPERTS = 16
N_EXPERT_GROUPS = 4
EXPERTS_PER_GROUP = N_EXPERTS // N_EXPERT_GROUPS
TOP_K = 2
EXPERT_FF = 512
N_MOD = 6
EPS = 1e-6

kernel_name = "hybrid_rglru_gqa_grouped_moe_dit"


def rms_norm(x, w):
    xf = x.astype(jnp.float32)
    y = xf * lax.rsqrt(jnp.mean(xf * xf, axis=-1, keepdims=True) + EPS)
    return (y * w.astype(jnp.float32)).astype(x.dtype)


def adaln(cond, w, b):
    m = jax.nn.silu(cond) @ w + b
    m = m.reshape((-1, 1, N_MOD * D_MODEL))
    return jnp.split(m, N_MOD, axis=-1)


def modulate(x, g, shift, scale):
    return rms_norm(x, g) * (1 + scale) + shift


def dwconv_centred(u, w, b):
    left = CONV_W // 2
    right = CONV_W - 1 - left
    L = u.shape[1]
    up = jnp.pad(u, ((0, 0), (left, right), (0, 0)))
    out = b
    for k in range(CONV_W):
        out = out + up[:, k:k + L] * w[k]
    return out


def rglru_coeffs(u, wa, ba, wx, bx, lam):
    B_, L, C = u.shape
    ub = u.reshape(B_, L, N_LRU_BLOCKS, LRU_BLOCK)
    r = jax.nn.sigmoid((jnp.einsum("blnd,nde->blne", ub, wa).reshape(B_, L, C) + ba).astype(jnp.float32))
    ig = jax.nn.sigmoid((jnp.einsum("blnd,nde->blne", ub, wx).reshape(B_, L, C) + bx).astype(jnp.float32))
    log_a = -LRU_C * r * jax.nn.softplus(-lam.astype(jnp.float32))
    a = jnp.exp(log_a)
    bterm = jnp.sqrt(-jnp.expm1(2.0 * log_a)) * (ig * u.astype(jnp.float32))
    return a, bterm


def linear_scan(a, b, h0, reverse):
    if h0 is not None:
        if reverse:
            b = b.at[:, -1].add(a[:, -1] * h0)
        else:
            b = b.at[:, 0].add(a[:, 0] * h0)

    def combine(e1, e2):
        a1, b1 = e1
        a2, b2 = e2
        return a1 * a2, a2 * b1 + b2

    _, h = lax.associative_scan(combine, (a, b), reverse=reverse, axis=1)
    return h


def rglru_mixer(hx, hc, in_w, conv_w, conv_b, ga_w, ga_b, gx_w, gx_b, lam, out_w, ctx_out):
    w_gate, w_in = in_w[:, :D_RNN], in_w[:, D_RNN:]
    gx_, ux = jnp.split(hx @ in_w, 2, axis=-1)
    uc = hc @ w_in
    ux = dwconv_centred(ux, conv_w, conv_b)
    uc = dwconv_centred(uc, conv_w, conv_b)
    yx = 0.0
    yc = 0.0
    for d in range(2):
        rev = d == 1
        a_c, b_c = rglru_coeffs(uc, ga_w[d], ga_b[d], gx_w[d], gx_b[d], lam[d])
        h_c = linear_scan(a_c, b_c, None, rev)
        h0 = h_c[:, 0] if rev else h_c[:, -1]
        a_x, b_x = rglru_coeffs(ux, ga_w[d], ga_b[d], gx_w[d], gx_b[d], lam[d])
        h_x = linear_scan(a_x, b_x, h0, rev)
        yx = yx + h_x
        if ctx_out:
            yc = yc + h_c
    out_x = (jax.nn.gelu(gx_) * yx.astype(hx.dtype)) @ out_w
    out_c = None
    if ctx_out:
        out_c = (jax.nn.gelu(hc @ w_gate) * yc.astype(hc.dtype)) @ out_w
    return out_x, out_c


def rope_1d(t, pos):
    half = t.shape[-1] // 2
    inv = ROPE_THETA ** (-jnp.arange(half, dtype=jnp.float32) / half)
    ang = pos.astype(jnp.float32)[:, None] * inv
    cos = jnp.cos(ang)[None, :, None, :]
    sin = jnp.sin(ang)[None, :, None, :]
    tf = t.astype(jnp.float32)
    t1, t2 = tf[..., :half], tf[..., half:]
    return jnp.concatenate([t1 * cos - t2 * sin, t2 * cos + t1 * sin], axis=-1).astype(t.dtype)


def rope_2d(t):
    L = t.shape[1]
    rows = L // GRID_W
    pos_r = jnp.repeat(jnp.arange(rows, dtype=jnp.int32), GRID_W)
    pos_c = jnp.tile(jnp.arange(GRID_W, dtype=jnp.int32), rows)
    half = HEAD_DIM // 2
    return jnp.concatenate([rope_1d(t[..., :half], pos_r), rope_1d(t[..., half:], pos_c)], axis=-1)


def attention_mixer(hx, hc, qkv_w, qn_w, kn_w, o_w, ctx_out):
    B_, L, _ = hx.shape
    C = hc.shape[1]
    q_cols = N_HEADS * HEAD_DIM
    kv_cols = N_KV_HEADS * HEAD_DIM
    scale = 1.0 / math.sqrt(HEAD_DIM)

    qx, kx, vx = jnp.split(hx @ qkv_w, [q_cols, q_cols + kv_cols], axis=-1)
    qx = rope_2d(rms_norm(qx.reshape(B_, L, N_HEADS, HEAD_DIM), qn_w))
    kx = rope_2d(rms_norm(kx.reshape(B_, L, N_KV_HEADS, HEAD_DIM), kn_w))
    vx = vx.reshape(B_, L, N_KV_HEADS, HEAD_DIM)

    if ctx_out:
        qc, kc, vc = jnp.split(hc @ qkv_w, [q_cols, q_cols + kv_cols], axis=-1)
    else:
        kc, vc = jnp.split(hc @ qkv_w[:, q_cols:], [kv_cols], axis=-1)
    kc = rms_norm(kc.reshape(B_, C, N_KV_HEADS, HEAD_DIM), kn_w)
    vc = vc.reshape(B_, C, N_KV_HEADS, HEAD_DIM)

    def attend(q, k, v):
        s = jnp.einsum("bqkgd,bskd->bkgqs", q, k).astype(jnp.float32) * scale
        p = jax.nn.softmax(s, axis=-1).astype(v.dtype)
        return jnp.einsum("bkgqs,bskd->bqkgd", p, v)

    k_all = jnp.concatenate([kc, kx], axis=1)
    v_all = jnp.concatenate([vc, vx], axis=1)
    n_blk = L // Q_BLOCK
    qb = qx.reshape(B_, n_blk, Q_BLOCK, N_KV_HEADS, KV_GROUP, HEAD_DIM).swapaxes(0, 1)
    ob = lax.map(lambda q: attend(q, k_all, v_all), qb)
    out_x = ob.swapaxes(0, 1).reshape(B_, L, q_cols) @ o_w

    out_c = None
    if ctx_out:
        qc = rms_norm(qc.reshape(B_, C, N_HEADS, HEAD_DIM), qn_w).reshape(B_, C, N_KV_HEADS, KV_GROUP, HEAD_DIM)
        out_c = attend(qc, kc, vc).reshape(B_, C, q_cols) @ o_w
    return out_x, out_c


def grouped_moe(h, router_w, router_b, w1, w3, w2):
    logits = (h @ router_w).astype(jnp.float32) + router_b.astype(jnp.float32)
    probs = jax.nn.softmax(logits, axis=-1)
    pg = probs.reshape(probs.shape[:-1] + (N_EXPERT_GROUPS, EXPERTS_PER_GROUP))
    group_score = lax.top_k(pg, TOP_K)[0].sum(-1)
    sel = jnp.argmax(group_score, axis=-1)
    in_group = (sel[..., None] == jnp.arange(N_EXPERT_GROUPS))[..., None]
    masked = jnp.where(in_group, pg, -1.0).reshape(probs.shape)
    top_w, top_i = lax.top_k(masked, TOP_K)
    top_w = top_w / jnp.sum(top_w, axis=-1, keepdims=True)
    gates = jnp.sum(jax.nn.one_hot(top_i, N_EXPERTS, dtype=jnp.float32) * top_w[..., None], axis=-2).astype(h.dtype)
    y = jnp.zeros_like(h)
    for e in range(N_EXPERTS):
        y = y + gates[..., e:e + 1] * ((jax.nn.silu(h @ w1[e]) * (h @ w3[e])) @ w2[e])
    return y


def setup_inputs(seed: int = 0) -> dict:
    key = jax.random.key(seed)
    ks = jax.random.split(key, 28)
    f32 = jnp.float32

    def nrm(k, shape, fan_in):
        return jax.random.normal(k, shape, f32) * fan_in ** -0.5

    def gain(k, shape):
        return 1.0 + 0.05 * jax.random.normal(k, shape, f32)

    def small(k, shape, s=0.02):
        return s * jax.random.normal(k, shape, f32)

    p_a = jax.random.uniform(ks[14], (N_LRU_LAYERS, 2, D_RNN), f32, minval=0.9, maxval=0.999) ** (1.0 / LRU_C)
    lam = jnp.log(p_a) - jnp.log1p(-p_a)
    return {
        "x": jax.random.normal(ks[0], (BATCH, SEQ, D_MODEL), f32),
        "c": jax.random.normal(ks[1], (BATCH, D_MODEL), f32),
        "ctx": jax.random.normal(ks[2], (BATCH, CTX_LEN, D_MODEL), f32),
        "c_ctx": jax.random.normal(ks[3], (D_MODEL,), f32),
        "ada_w": nrm(ks[4], (DEPTH, D_MODEL, N_MOD * D_MODEL), D_MODEL),
        "ada_b": small(ks[5], (DEPTH, N_MOD * D_MODEL)),
        "norm_mix_w": gain(ks[6], (DEPTH, D_MODEL)),
        "norm_ffn_w": gain(ks[7], (DEPTH, D_MODEL)),
        "lru_in_w": nrm(ks[8], (N_LRU_LAYERS, D_MODEL, 2 * D_RNN), D_MODEL),
        "lru_conv_w": nrm(ks[9], (N_LRU_LAYERS, CONV_W, D_RNN), CONV_W),
        "lru_conv_b": small(ks[10], (N_LRU_LAYERS, D_RNN)),
        "lru_gate_a_w": nrm(ks[11], (N_LRU_LAYERS, 2, N_LRU_BLOCKS, LRU_BLOCK, LRU_BLOCK), LRU_BLOCK),
        "lru_gate_a_b": small(ks[12], (N_LRU_LAYERS, 2, D_RNN)),
        "lru_gate_x_w": nrm(ks[13], (N_LRU_LAYERS, 2, N_LRU_BLOCKS, LRU_BLOCK, LRU_BLOCK), LRU_BLOCK),
        "lru_gate_x_b": small(ks[15], (N_LRU_LAYERS, 2, D_RNN)),
        "lru_lambda": lam,
        "lru_out_w": nrm(ks[16], (N_LRU_LAYERS, D_RNN, D_MODEL), D_RNN),
        "attn_qkv_w": nrm(ks[17], (N_ATTN_LAYERS, D_MODEL, (N_HEADS + 2 * N_KV_HEADS) * HEAD_DIM), D_MODEL),
        "attn_q_norm_w": gain(ks[18], (N_ATTN_LAYERS, HEAD_DIM)),
        "attn_k_norm_w": gain(ks[19], (N_ATTN_LAYERS, HEAD_DIM)),
        "attn_o_w": nrm(ks[20], (N_ATTN_LAYERS, N_HEADS * HEAD_DIM, D_MODEL), N_HEADS * HEAD_DIM),
        "router_w": nrm(ks[21], (D_MODEL, N_EXPERTS), D_MODEL),
        "router_b": small(ks[22], (N_EXPERTS,), 0.01),
        "moe_w1": nrm(ks[23], (DEPTH, N_EXPERTS, D_MODEL, EXPERT_FF), D_MODEL),
        "moe_w3": nrm(ks[24], (DEPTH, N_EXPERTS, D_MODEL, EXPERT_FF), D_MODEL),
        "moe_w2": nrm(ks[25], (DEPTH, N_EXPERTS, EXPERT_FF, D_MODEL), EXPERT_FF),
    }


def reference(x, c, ctx, c_ctx, ada_w, ada_b, norm_mix_w, norm_ffn_w,
              lru_in_w, lru_conv_w, lru_conv_b, lru_gate_a_w, lru_gate_a_b, lru_gate_x_w, lru_gate_x_b,
              lru_lambda, lru_out_w, attn_qkv_w, attn_q_norm_w, attn_k_norm_w, attn_o_w,
              router_w, router_b, moe_w1, moe_w3, moe_w2):
    C = ctx.shape[1]
    for i in range(DEPTH):
        ctx_out = i < DEPTH - 1
        j = i // N_MIXERS
        sh_m, sc_m, g_m, sh_f, sc_f, g_f = adaln(c, ada_w[i], ada_b[i])
        csh_m, csc_m, cg_m, csh_f, csc_f, cg_f = adaln(c_ctx, ada_w[i], ada_b[i])

        hx = modulate(x, norm_mix_w[i], sh_m, sc_m)
        hc = modulate(ctx, norm_mix_w[i], csh_m, csc_m)
        if i % N_MIXERS == 0:
            dx, dc = rglru_mixer(hx, hc, lru_in_w[j], lru_conv_w[j], lru_conv_b[j],
                                 lru_gate_a_w[j], lru_gate_a_b[j], lru_gate_x_w[j], lru_gate_x_b[j],
                                 lru_lambda[j], lru_out_w[j], ctx_out)
        else:
            dx, dc = attention_mixer(hx, hc, attn_qkv_w[j], attn_q_norm_w[j], attn_k_norm_w[j],
                                     attn_o_w[j], ctx_out)
        x = x + g_m * dx
        if ctx_out:
            ctx = ctx + cg_m * dc

        hx = modulate(x, norm_ffn_w[i], sh_f, sc_f)
        if ctx_out:
            hc = modulate(ctx, norm_ffn_w[i], csh_f, csc_f)
            y = grouped_moe(jnp.concatenate([hc, hx], axis=1), router_w, router_b,
                            moe_w1[i], moe_w3[i], moe_w2[i])
            ctx = ctx + cg_f * y[:, :C]
            x = x + g_f * y[:, C:]
        else:
            x = x + g_f * grouped_moe(hx, router_w, router_b, moe_w1[i], moe_w3[i], moe_w2[i])
    return x
```

```python
import functools
import math

import jax
import jax.numpy as jnp
from jax import lax
from jax.experimental import pallas as pl
from jax.experimental.pallas import tpu as pltpu

F32 = jnp.float32
BF16 = jnp.bfloat16
HIGHEST = lax.Precision.HIGHEST

GRID_W = 64
N_LRU_BLOCKS = 8
CONV_W = 4
LRU_C = 8.0
HEAD_DIM = 64
N_KV_HEADS = 4
ROPE_THETA = 10000.0
N_EXPERT_GROUPS = 4
N_MOD = 6
EPS = 1e-6

LANES = 128
SUBLANES = 8
VMEM_LIMIT = 48 * 1024 * 1024

TM_PROJ = 512
TC_SCAN = 512
TQ_ATTN = 256
TM_MOE = 1024


def _cparams(sem):
    return pltpu.CompilerParams(dimension_semantics=sem, vmem_limit_bytes=VMEM_LIMIT)


def _sigmoid(x):
    return 1.0 / (1.0 + jnp.exp(-x))


def _gelu_tanh(x):
    c = math.sqrt(2.0 / math.pi)
    return x * (0.5 * (1.0 + jnp.tanh(c * (x + 0.044715 * (x * x * x)))))


def _modulate(xf, g, shift, scale):
    ms = jnp.mean(xf * xf, axis=-1, keepdims=True)
    y = xf * lax.rsqrt(ms + EPS) * g
    return y * (1.0 + scale) + shift


def _mod_spec(mod, d):
    if mod.shape[0] == 1:
        return pl.BlockSpec((None, N_MOD, d), lambda b, i: (0, 0, 0))
    return pl.BlockSpec((None, N_MOD, d), lambda b, i: (b, 0, 0))


def _ada_kernel(cond_ref, w_ref, b_ref, o_ref):
    cnd = cond_ref[...]
    s = cnd * _sigmoid(cnd)
    o_ref[...] = jnp.dot(s, w_ref[...], precision=HIGHEST, preferred_element_type=F32) + b_ref[...]


def _ada(cond, ada_w, ada_b):
    depth, d, n = ada_w.shape
    rows = cond.shape[0]
    tn = 1536
    return pl.pallas_call(
        _ada_kernel,
        grid=(depth, n // tn),
        in_specs=[
            pl.BlockSpec((rows, d), lambda l, j: (0, 0)),
            pl.BlockSpec((None, d, tn), lambda l, j: (l, 0, j)),
            pl.BlockSpec((None, 1, tn), lambda l, j: (l, 0, j)),
        ],
        out_specs=pl.BlockSpec((None, rows, tn), lambda l, j: (l, 0, j)),
        out_shape=jax.ShapeDtypeStruct((depth, rows, n), F32),
        compiler_params=_cparams(("arbitrary", "arbitrary")),
        name="ada",
    )(cond, ada_w, ada_b.reshape(depth, 1, n))


def _inproj_kernel(x_ref, mod_ref, nw_ref, w_ref, g_ref, u_ref):
    d = x_ref.shape[-1]
    h = _modulate(x_ref[...], nw_ref[...], mod_ref[0:1, :], mod_ref[1:2, :])
    r = jnp.dot(h.astype(BF16), w_ref[...], preferred_element_type=F32)
    g_ref[...] = _gelu_tanh(r[:, :d]).astype(BF16)
    u_ref[...] = r[:, d:]


def _inproj(x, mod, norm_w, w_bf):
    b, s, d = x.shape
    tm = min(TM_PROJ, s)
    row = pl.BlockSpec((None, tm, d), lambda bi, i: (bi, i, 0))
    return pl.pallas_call(
        _inproj_kernel,
        grid=(b, s // tm),
        in_specs=[
            row,
            _mod_spec(mod, d),
            pl.BlockSpec((1, d), lambda bi, i: (0, 0)),
            pl.BlockSpec((d, 2 * d), lambda bi, i: (0, 0)),
        ],
        out_specs=[row, row],
        out_shape=[jax.ShapeDtypeStruct((b, s, d), BF16), jax.ShapeDtypeStruct((b, s, d), F32)],
        compiler_params=_cparams(("parallel", "parallel")),
        name="lru_inproj",
    )(x, mod, norm_w.reshape(1, d), w_bf)


def _scan_kernel(u_ref, up_ref, un_ref, h0_ref, cw_ref, cb_ref, wg_ref, bg_ref, lam_ref,
                 o_ref, hfin_ref, ubuf, a_s, b_s, carry):
    tc, d = u_ref.shape
    direction = pl.program_id(1)
    j = pl.program_id(2)
    n = pl.num_programs(2)
    chunk = jnp.where(direction == 0, j, n - 1 - j)

    @pl.when(j == 0)
    def _():
        carry[...] = h0_ref[...]

    ubuf[0:SUBLANES, :] = jnp.where(chunk > 0, up_ref[...], 0.0)
    ubuf[SUBLANES:SUBLANES + tc, :] = u_ref[...]
    ubuf[SUBLANES + tc:2 * SUBLANES + tc, :] = jnp.where(chunk < n - 1, un_ref[...], 0.0)
    left = CONV_W // 2
    uc = cb_ref[...]
    for k in range(CONV_W):
        uc = uc + ubuf[pl.ds(SUBLANES - left + k, tc), :] * cw_ref[k:k + 1, :]

    neg_lam = -lam_ref[...]
    softplus = jnp.maximum(neg_lam, 0.0) + jnp.log1p(jnp.exp(-jnp.abs(neg_lam)))
    decay = -LRU_C * softplus
    ub = uc.astype(BF16)
    blk = d // N_LRU_BLOCKS
    for nb in range(N_LRU_BLOCKS):
        sl = slice(blk * nb, blk * (nb + 1))
        z = jnp.dot(ub[:, sl], wg_ref[nb], preferred_element_type=F32)
        r = _sigmoid(z[:, :blk] + bg_ref[0:1, sl])
        ig = _sigmoid(z[:, blk:] + bg_ref[1:2, sl])
        log_a = decay[:, sl] * r
        a = jnp.exp(log_a)
        a_s[:, sl] = a
        b_s[:, sl] = jnp.sqrt(-jnp.tanh(log_a) * (a * a + 1.0)) * (ig * uc[:, sl])

    row = lax.broadcasted_iota(jnp.int32, (SUBLANES, d), 0)
    n_tiles = tc // SUBLANES

    def tile_scan(r0, h, reverse):
        a = a_s[pl.ds(r0, SUBLANES), :]
        bv = b_s[pl.ds(r0, SUBLANES), :]
        for s in (1, 2, 4):
            if reverse:
                valid = row < SUBLANES - s
                shift = SUBLANES - s
            else:
                valid = row >= s
                shift = s
            a_sh = jnp.where(valid, pltpu.roll(a, shift, 0), 1.0)
            b_sh = jnp.where(valid, pltpu.roll(bv, shift, 0), 0.0)
            bv = a * b_sh + bv
            a = a * a_sh
        hh = a * h + bv
        o_ref[pl.ds(r0, SUBLANES), :] = hh
        return hh[0:1, :] if reverse else hh[SUBLANES - 1:SUBLANES, :]

    @pl.when(direction == 0)
    def _():
        def body(i, h):
            return tile_scan(pl.multiple_of(i * SUBLANES, SUBLANES), h, False)
        carry[...] = lax.fori_loop(0, n_tiles, body, carry[...], unroll=4)

    @pl.when(direction == 1)
    def _():
        def body(i, h):
            return tile_scan(pl.multiple_of((n_tiles - 1 - i) * SUBLANES, SUBLANES), h, True)
        carry[...] = lax.fori_loop(0, n_tiles, body, carry[...], unroll=4)

    hfin_ref[...] = carry[...]


def _lru_scan(u, h0, conv_w, conv_b, wg_bf, bg, lam):
    b, s, d = u.shape
    tc = min(TC_SCAN, s)
    n = s // tc
    per = tc // SUBLANES
    last8 = s // SUBLANES - 1
    blk = d // N_LRU_BLOCKS

    def chunk_of(di, j):
        return j + di * (n - 1 - 2 * j)

    return pl.pallas_call(
        _scan_kernel,
        grid=(b, 2, n),
        in_specs=[
            pl.BlockSpec((None, tc, d), lambda bi, di, j: (bi, chunk_of(di, j), 0)),
            pl.BlockSpec((None, SUBLANES, d),
                         lambda bi, di, j: (bi, jnp.maximum(chunk_of(di, j) * per - 1, 0), 0)),
            pl.BlockSpec((None, SUBLANES, d),
                         lambda bi, di, j: (bi, jnp.minimum((chunk_of(di, j) + 1) * per, last8), 0)),
            pl.BlockSpec((None, None, 1, d), lambda bi, di, j: (di, bi, 0, 0)),
            pl.BlockSpec((CONV_W, d), lambda bi, di, j: (0, 0)),
            pl.BlockSpec((1, d), lambda bi, di, j: (0, 0)),
            pl.BlockSpec((None, N_LRU_BLOCKS, blk, 2 * blk), lambda bi, di, j: (di, 0, 0, 0)),
            pl.BlockSpec((None, 2, d), lambda bi, di, j: (di, 0, 0)),
            pl.BlockSpec((None, 1, d), lambda bi, di, j: (di, 0, 0)),
        ],
        out_specs=[
            pl.BlockSpec((None, None, tc, d), lambda bi, di, j: (di, bi, chunk_of(di, j), 0)),
            pl.BlockSpec((None, None, 1, d), lambda bi, di, j: (di, bi, 0, 0)),
        ],
        out_shape=[jax.ShapeDtypeStruct((2, b, s, d), F32), jax.ShapeDtypeStruct((2, b, 1, d), F32)],
        scratch_shapes=[
            pltpu.VMEM((tc + 2 * SUBLANES, d), F32),
            pltpu.VMEM((tc, d), F32),
            pltpu.VMEM((tc, d), F32),
            pltpu.VMEM((1, d), F32),
        ],
        compiler_params=_cparams(("arbitrary", "arbitrary", "arbitrary")),
        name="lru_scan",
    )(u, u, u, h0, conv_w, conv_b.reshape(1, d), wg_bf, bg, lam.reshape(2, 1, d))


def _lru_out_kernel(g_ref, hf_ref, hb_ref, x_ref, mod_ref, w_ref, o_ref):
    y = hf_ref[...] + hb_ref[...]
    t = (g_ref[...].astype(F32) * y).astype(BF16)
    o = jnp.dot(t, w_ref[...], preferred_element_type=F32)
    o_ref[...] = x_ref[...] + mod_ref[2:3, :] * o


def _lru_out(g, h, x, mod, w_bf):
    b, s, d = x.shape
    tm = min(TM_PROJ, s)
    row = pl.BlockSpec((None, tm, d), lambda bi, i: (bi, i, 0))
    return pl.pallas_call(
        _lru_out_kernel,
        grid=(b, s // tm),
        in_specs=[
            row,
            pl.BlockSpec((None, None, tm, d), lambda bi, i: (0, bi, i, 0)),
            pl.BlockSpec((None, None, tm, d), lambda bi, i: (1, bi, i, 0)),
            row,
            _mod_spec(mod, d),
            pl.BlockSpec((d, d), lambda bi, i: (0, 0)),
        ],
        out_specs=row,
        out_shape=jax.ShapeDtypeStruct((b, s, d), F32),
        compiler_params=_cparams(("parallel", "parallel")),
        name="lru_out",
    )(g, h, h, x, mod, w_bf)


def _attn_out_kernel(a_ref, x_ref, mod_ref, w_ref, o_ref):
    o = jnp.dot(a_ref[...], w_ref[...], preferred_element_type=F32)
    o_ref[...] = x_ref[...] + mod_ref[2:3, :] * o


def _attn_out(a, x, mod, w_bf):
    b, s, d = x.shape
    tm = min(TM_PROJ, s)
    row = pl.BlockSpec((None, tm, d), lambda bi, i: (bi, i, 0))
    return pl.pallas_call(
        _attn_out_kernel,
        grid=(b, s // tm),
        in_specs=[row, row, _mod_spec(mod, d), pl.BlockSpec((d, d), lambda bi, i: (0, 0))],
        out_specs=row,
        out_shape=jax.ShapeDtypeStruct((b, s, d), F32),
        compiler_params=_cparams(("parallel", "parallel")),
        name="attn_out",
    )(a, x, mod, w_bf)


def _first_hit(vals, target):
    hits = []
    found = None
    for v in vals:
        eq = v == target
        if found is None:
            hits.append(eq)
            found = eq
        else:
            hits.append(jnp.logical_and(eq, jnp.logical_not(found)))
            found = jnp.logical_or(found, eq)
    return hits


def _route_kernel(x_ref, mod_ref, nw_ref, rwt_ref, rb_ref, h_ref, gt_ref):
    n_exp = rwt_ref.shape[0]
    per_group = n_exp // N_EXPERT_GROUPS
    h = _modulate(x_ref[...], nw_ref[...], mod_ref[3:4, :], mod_ref[4:5, :])
    h_ref[...] = h.astype(BF16)
    logits = lax.dot_general(rwt_ref[...], h, (((1,), (1,)), ((), ())),
                             precision=HIGHEST, preferred_element_type=F32) + rb_ref[...]
    rows = [logits[e:e + 1, :] for e in range(n_exp)]
    mx = functools.reduce(jnp.maximum, rows)
    ex = [jnp.exp(r - mx) for r in rows]
    z = functools.reduce(lambda p, q: p + q, ex)
    probs = [e_ / z for e_ in ex]
    scores = []
    for g in range(N_EXPERT_GROUPS):
        v0, v1, v2, v3 = probs[per_group * g:per_group * (g + 1)]
        hi01, lo01 = jnp.maximum(v0, v1), jnp.minimum(v0, v1)
        hi23, lo23 = jnp.maximum(v2, v3), jnp.minimum(v2, v3)
        top1 = jnp.maximum(hi01, hi23)
        top2 = jnp.maximum(jnp.minimum(hi01, hi23), jnp.maximum(lo01, lo23))
        scores.append(top1 + top2)
    best = functools.reduce(jnp.maximum, scores)
    sel = _first_hit(scores, best)
    masked = [jnp.where(sel[e // per_group], probs[e], -1.0) for e in range(n_exp)]
    m1 = functools.reduce(jnp.maximum, masked)
    is1 = _first_hit(masked, m1)
    masked2 = [jnp.where(is1[e], -2.0, masked[e]) for e in range(n_exp)]
    m2 = functools.reduce(jnp.maximum, masked2)
    is2 = _first_hit(masked2, m2)
    denom = m1 + m2
    for e in range(n_exp):
        gt_ref[e:e + 1, :] = jnp.where(is1[e], m1, jnp.where(is2[e], m2, 0.0)) / denom


def _route(x, mod, norm_w, router_w, router_b):
    b, s, d = x.shape
    n_exp = router_w.shape[1]
    tm = min(TM_PROJ, s)
    n_s = s // tm
    row = pl.BlockSpec((None, tm, d), lambda bi, i: (bi, i, 0))
    return pl.pallas_call(
        _route_kernel,
        grid=(b, n_s),
        in_specs=[
            row,
            _mod_spec(mod, d),
            pl.BlockSpec((1, d), lambda bi, i: (0, 0)),
            pl.BlockSpec((n_exp, d), lambda bi, i: (0, 0)),
            pl.BlockSpec((n_exp, 1), lambda bi, i: (0, 0)),
        ],
        out_specs=[row, pl.BlockSpec((n_exp, tm), lambda bi, i: (0, bi * n_s + i))],
        out_shape=[jax.ShapeDtypeStruct((b, s, d), BF16), jax.ShapeDtypeStruct((n_exp, b * s), F32)],
        compiler_params=_cparams(("parallel", "parallel")),
        name="moe_route",
    )(x, mod, norm_w.reshape(1, d), router_w.T, router_b.reshape(n_exp, 1))


def _moe_kernel(h_ref, gates_ref, x_ref, mod_ref, w1_ref, w3_ref, w2_ref, o_ref, acc):
    e = pl.program_id(1)
    h = h_ref[...]
    a = jnp.dot(h, w1_ref[...], preferred_element_type=F32)
    c = jnp.dot(h, w3_ref[...], preferred_element_type=F32)
    lane = lax.broadcasted_iota(jnp.int32, gates_ref.shape, 1)
    gate = jnp.sum(jnp.where(lane == e, gates_ref[...], 0.0), axis=1, keepdims=True)
    t = ((a * _sigmoid(a)) * c * gate).astype(BF16)
    y = jnp.dot(t, w2_ref[...], preferred_element_type=F32)

    @pl.when(e == 0)
    def _():
        acc[...] = y

    @pl.when(e > 0)
    def _():
        acc[...] += y

    @pl.when(e == pl.num_programs(1) - 1)
    def _():
        o_ref[...] = x_ref[...] + mod_ref[5:6, :] * acc[...]


def _moe(h, gates, x, mod, w1_bf, w3_bf, w2_bf):
    b, s, d = x.shape
    n_exp, _, ff = w1_bf.shape
    n_tok = b * s
    if mod.shape[0] == 1:
        tm = min(TM_MOE, n_tok)
        mod_spec = pl.BlockSpec((None, N_MOD, d), lambda i, e: (0, 0, 0))
    else:
        tm = min(TM_MOE, s)
        mod_spec = pl.BlockSpec((None, N_MOD, d), lambda i, e: ((i * tm) // s, 0, 0))
    row = pl.BlockSpec((tm, d), lambda i, e: (i, 0))
    out = pl.pallas_call(
        _moe_kernel,
        grid=(n_tok // tm, n_exp),
        in_specs=[
            row,
            pl.BlockSpec((tm, n_exp), lambda i, e: (i, 0)),
            row,
            mod_spec,
            pl.BlockSpec((None, d, ff), lambda i, e: (e, 0, 0)),
            pl.BlockSpec((None, d, ff), lambda i, e: (e, 0, 0)),
            pl.BlockSpec((None, ff, d), lambda i, e: (e, 0, 0)),
        ],
        out_specs=row,
        out_shape=jax.ShapeDtypeStruct((n_tok, d), F32),
        scratch_shapes=[pltpu.VMEM((tm, d), F32)],
        compiler_params=_cparams(("parallel", "arbitrary")),
        name="moe_experts",
    )(h.reshape(n_tok, d), gates, x.reshape(n_tok, d), mod, w1_bf, w3_bf, w2_bf)
    return out.reshape(b, s, d)


def _moe_block(x, mod, norm_w, router_w, router_b, w1_bf, w3_bf, w2_bf):
    h, gates_t = _route(x, mod, norm_w, router_w, router_b)
    return _moe(h, gates_t.T, x, mod, w1_bf, w3_bf, w2_bf)


def _head_norm(t, w_row, ones_blockdiag):
    sq = t * t
    hi = sq.astype(BF16)
    lo = (sq - hi.astype(F32)).astype(BF16)
    ss = (jnp.dot(hi, ones_blockdiag, preferred_element_type=F32)
          + jnp.dot(lo, ones_blockdiag, preferred_element_type=F32))
    return t * lax.rsqrt(ss * (1.0 / HEAD_DIM) + EPS) * w_row


def _rope(t, cos, sin_next, sin_prev):
    quarter = HEAD_DIM // 4
    outs = []
    for nb in range(t.shape[1] // LANES):
        tb = t[:, LANES * nb:LANES * (nb + 1)]
        outs.append(tb * cos
                    + pltpu.roll(tb, LANES - quarter, 1) * sin_next
                    + pltpu.roll(tb, quarter, 1) * sin_prev)
    return jnp.concatenate(outs, axis=1)


def _qkv_kernel(x_ref, mod_ref, nw_ref, w_ref, qn_ref, kn_ref, ones_ref, cos_ref, sn_ref, sp_ref,
                q_ref, k_ref, v_ref):
    n_q = q_ref.shape[-1]
    n_kv = k_ref.shape[-1]
    h = _modulate(x_ref[...], nw_ref[...], mod_ref[0:1, :], mod_ref[1:2, :])
    r = jnp.dot(h.astype(BF16), w_ref[...], preferred_element_type=F32)
    cos, sn, sp = cos_ref[...], sn_ref[...], sp_ref[...]
    q = _head_norm(r[:, :n_q], qn_ref[...], ones_ref[...])
    q_ref[...] = (_rope(q, cos, sn, sp) * (1.0 / math.sqrt(HEAD_DIM))).astype(BF16)
    k = _head_norm(r[:, n_q:n_q + n_kv], kn_ref[...], ones_ref[0:n_kv, 0:n_kv])
    k_ref[...] = _rope(k, cos, sn, sp).astype(BF16)
    v_ref[...] = r[:, n_q + n_kv:].astype(BF16)


def _kv_ctx_kernel(x_ref, mod_ref, nw_ref, w_ref, kn_ref, ones_ref, k_ref, v_ref):
    n_kv = k_ref.shape[-1]
    h = _modulate(x_ref[...], nw_ref[...], mod_ref[0:1, :], mod_ref[1:2, :])
    r = jnp.dot(h.astype(BF16), w_ref[...], preferred_element_type=F32)
    k_ref[...] = _head_norm(r[:, :n_kv], kn_ref[...], ones_ref[...]).astype(BF16)
    v_ref[...] = r[:, n_kv:].astype(BF16)


def _rope_tables(s):
    half = HEAD_DIM // 2
    quarter = half // 2
    t = jnp.arange(s, dtype=jnp.int32)
    pos_r = (t // GRID_W).astype(F32)[:, None]
    pos_c = (t % GRID_W).astype(F32)[:, None]
    lane = jnp.arange(LANES, dtype=jnp.int32)[None, :]
    in_head = lane % HEAD_DIM
    in_half = in_head % half
    freq = (in_half % quarter).astype(F32)
    inv = ROPE_THETA ** (-freq / quarter)
    ang = jnp.where(in_head < half, pos_r, pos_c) * inv
    cos = jnp.cos(ang)
    sin = jnp.sin(ang)
    first = in_half < quarter
    return cos, jnp.where(first, -sin, 0.0), jnp.where(first, 0.0, sin)


def _ones_blockdiag(n):
    i = jnp.arange(n, dtype=jnp.int32)
    return (i[:, None] // HEAD_DIM == i[None, :] // HEAD_DIM).astype(BF16)


def _qkv(x, mod, norm_w, w_bf, qn_w, kn_w):
    b, s, d = x.shape
    n_kv = N_KV_HEADS * HEAD_DIM
    n_q = w_bf.shape[1] - 2 * n_kv
    tm = min(TM_PROJ, s)
    cos, sn, sp = _rope_tables(s)
    tab = pl.BlockSpec((tm, LANES), lambda i, bi: (i, 0))

    def row(w):
        return pl.BlockSpec((None, tm, w), lambda i, bi: (bi, i, 0))

    def const(shape):
        return pl.BlockSpec(shape, lambda i, bi: (0,) * len(shape))

    return pl.pallas_call(
        _qkv_kernel,
        grid=(s // tm, b),
        in_specs=[
            row(d),
            pl.BlockSpec((None, N_MOD, d), lambda i, bi: (bi, 0, 0)),
            const((1, d)),
            const((d, n_q + 2 * n_kv)),
            const((1, n_q)),
            const((1, n_kv)),
            const((n_q, n_q)),
            tab, tab, tab,
        ],
        out_specs=[row(n_q), row(n_kv), row(n_kv)],
        out_shape=[jax.ShapeDtypeStruct((b, s, n_q), BF16),
                   jax.ShapeDtypeStruct((b, s, n_kv), BF16),
                   jax.ShapeDtypeStruct((b, s, n_kv), BF16)],
        compiler_params=_cparams(("parallel", "parallel")),
        name="attn_qkv",
    )(x, mod, norm_w.reshape(1, d), w_bf,
      jnp.tile(qn_w, n_q // HEAD_DIM).reshape(1, n_q), jnp.tile(kn_w, N_KV_HEADS).reshape(1, n_kv),
      _ones_blockdiag(n_q), cos, sn, sp)


def _kv_ctx(ctx, mod, norm_w, w_kv_bf, kn_w):
    b, s, d = ctx.shape
    n_kv = N_KV_HEADS * HEAD_DIM
    tm = min(TM_PROJ, s)

    def row(w):
        return pl.BlockSpec((None, tm, w), lambda bi, i: (bi, i, 0))

    def const(shape):
        return pl.BlockSpec(shape, lambda bi, i: (0,) * len(shape))

    return pl.pallas_call(
        _kv_ctx_kernel,
        grid=(b, s // tm),
        in_specs=[row(d), _mod_spec(mod, d), const((1, d)), const((d, 2 * n_kv)),
                  const((1, n_kv)), const((n_kv, n_kv))],
        out_specs=[row(n_kv), row(n_kv)],
        out_shape=[jax.ShapeDtypeStruct((b, s, n_kv), BF16), jax.ShapeDtypeStruct((b, s, n_kv), BF16)],
        compiler_params=_cparams(("parallel", "parallel")),
        name="attn_kv_ctx",
    )(ctx, mod, norm_w.reshape(1, d), w_kv_bf, jnp.tile(kn_w, N_KV_HEADS).reshape(1, n_kv),
      _ones_blockdiag(n_kv))


def _attn_kernel(q_ref, kt_ref, v_ref, o_ref):
    q = q_ref[...]
    kt = kt_ref[...]
    v = v_ref[...]
    outs = []
    for j in range(q.shape[1] // HEAD_DIM):
        s = jnp.dot(q[:, HEAD_DIM * j:HEAD_DIM * (j + 1)], kt, preferred_element_type=F32)
        m = jnp.max(s, axis=-1, keepdims=True)
        p = jnp.exp(s - m)
        l = jnp.sum(p, axis=-1, keepdims=True)
        o = jnp.dot(p.astype(BF16), v, preferred_element_type=F32)
        outs.append(o / l)
    o_ref[...] = jnp.concatenate(outs, axis=1).astype(BF16)


def _attention(q, kt, v):
    b, s, n_q = q.shape
    n_keys = kt.shape[-1]
    group_w = n_q // N_KV_HEADS
    tq = min(TQ_ATTN, s)
    return pl.pallas_call(
        _attn_kernel,
        grid=(b, N_KV_HEADS, s // tq),
        in_specs=[
            pl.BlockSpec((None, tq, group_w), lambda bi, g, i: (bi, i, g)),
            pl.BlockSpec((None, None, HEAD_DIM, n_keys), lambda bi, g, i: (bi, g, 0, 0)),
            pl.BlockSpec((None, None, n_keys, HEAD_DIM), lambda bi, g, i: (bi, g, 0, 0)),
        ],
        out_specs=pl.BlockSpec((None, tq, group_w), lambda bi, g, i: (bi, i, g)),
        out_shape=jax.ShapeDtypeStruct((b, s, n_q), BF16),
        compiler_params=_cparams(("parallel", "parallel", "parallel")),
        name="attn_core",
    )(q, kt, v)


def _modulations(c, c_ctx, ada_w, ada_b):
    b, d = c.shape
    depth = ada_w.shape[0]
    assert b <= SUBLANES
    cond = jnp.zeros((2 * SUBLANES, d), F32).at[:b].set(c).at[b].set(c_ctx)
    mods = _ada(cond, ada_w, ada_b)
    mod_x = [mods[l, :b].reshape(b, N_MOD, d) for l in range(depth)]
    mod_c = [mods[l, b:b + 1].reshape(1, N_MOD, d) for l in range(depth)]
    return mod_x, mod_c


def _lru_layer(x, ctx, mod_x, mod_c, norm_w, in_w, conv_w, conv_b, ga_w, ga_b, gx_w, gx_b, lam, out_w):
    b, _, d = x.shape
    in_bf = in_w.astype(BF16)
    out_bf = out_w.astype(BF16)
    wg = jnp.concatenate([ga_w, gx_w], axis=-1).astype(BF16)
    bg = jnp.stack([ga_b, gx_b], axis=1)
    g_c, u_c = _inproj(ctx, mod_c, norm_w, in_bf)
    g_x, u_x = _inproj(x, mod_x, norm_w, in_bf)
    h_c, h_c_end = _lru_scan(u_c, jnp.zeros((2, b, 1, d), F32), conv_w, conv_b, wg, bg, lam)
    h_x, _ = _lru_scan(u_x, h_c_end, conv_w, conv_b, wg, bg, lam)
    return _lru_out(g_x, h_x, x, mod_x, out_bf), _lru_out(g_c, h_c, ctx, mod_c, out_bf)


def _attn_layer(x, ctx, mod_x, mod_c, norm_w, qkv_w, qn_w, kn_w, o_w):
    b, s, _ = x.shape
    n_kv = N_KV_HEADS * HEAD_DIM
    qkv_bf = qkv_w.astype(BF16)
    n_q = qkv_bf.shape[1] - 2 * n_kv
    q, k_x, v_x = _qkv(x, mod_x, norm_w, qkv_bf, qn_w, kn_w)
    k_c, v_c = _kv_ctx(ctx, mod_c, norm_w, qkv_bf[:, n_q:], kn_w)
    n_keys = ctx.shape[1] + s
    k_all = jnp.concatenate([k_c, k_x], axis=1).reshape(b, n_keys, N_KV_HEADS, HEAD_DIM)
    v_all = jnp.concatenate([v_c, v_x], axis=1).reshape(b, n_keys, N_KV_HEADS, HEAD_DIM)
    att = _attention(q, k_all.transpose(0, 2, 3, 1), v_all.transpose(0, 2, 1, 3))
    return _attn_out(att, x, mod_x, o_w.astype(BF16))


def kernel(x, c, ctx, c_ctx, ada_w, ada_b, norm_mix_w, norm_ffn_w, lru_in_w, lru_conv_w, lru_conv_b, lru_gate_a_w, lru_gate_a_b, lru_gate_x_w, lru_gate_x_b, lru_lambda, lru_out_w, attn_qkv_w, attn_q_norm_w, attn_k_norm_w, attn_o_w, router_w, router_b, moe_w1, moe_w3, moe_w2):
    assert ada_w.shape[0] == 2
    mod_x, mod_c = _modulations(c, c_ctx, ada_w, ada_b)

    x, ctx = _lru_layer(x, ctx, mod_x[0], mod_c[0], norm_mix_w[0], lru_in_w[0], lru_conv_w[0], lru_conv_b[0],
                        lru_gate_a_w[0], lru_gate_a_b[0], lru_gate_x_w[0], lru_gate_x_b[0], lru_lambda[0],
                        lru_out_w[0])
    w1, w3, w2 = moe_w1[0].astype(BF16), moe_w3[0].astype(BF16), moe_w2[0].astype(BF16)
    ctx = _moe_block(ctx, mod_c[0], norm_ffn_w[0], router_w, router_b, w1, w3, w2)
    x = _moe_block(x, mod_x[0], norm_ffn_w[0], router_w, router_b, w1, w3, w2)

    x = _attn_layer(x, ctx, mod_x[1], mod_c[1], norm_mix_w[1], attn_qkv_w[0], attn_q_norm_w[0],
                    attn_k_norm_w[0], attn_o_w[0])
    w1, w3, w2 = moe_w1[1].astype(BF16), moe_w3[1].astype(BF16), moe_w2[1].astype(BF16)
    return _moe_block(x, mod_x[1], norm_ffn_w[1], router_w, router_b, w1, w3, w2)
```

```python
import functools
import math

import jax
import jax.numpy as jnp
from jax import lax
from jax.experimental import pallas as pl
from jax.experimental.pallas import tpu as pltpu

F32 = jnp.float32
BF16 = jnp.bfloat16
HIGHEST = lax.Precision.HIGHEST

GRID_W = 64
N_LRU_BLOCKS = 8
CONV_W = 4
LRU_C = 8.0
HEAD_DIM = 64
N_KV_HEADS = 4
ROPE_THETA = 10000.0
N_EXPERT_GROUPS = 4
N_MOD = 6
EPS = 1e-6

LANES = 128
SUBLANES = 8
BF16_SUBLANES = 16
VMEM_LIMIT = 48 * 1024 * 1024
VMEM_LIMIT_MOE = 56 * 1024 * 1024

TM_PROJ = 512
TC_SCAN = 512
TQ_ATTN = 256
ATTN_SLAB = 256
TM_MOE = 1024
MOE_BLK = 128
MOE_CHUNK = 256


def _cparams(sem, vmem_limit=VMEM_LIMIT):
    return pltpu.CompilerParams(dimension_semantics=sem, vmem_limit_bytes=vmem_limit)


def _sigmoid(x):
    return 1.0 / (1.0 + jnp.exp(-x))


def _gelu_tanh(x):
    c = math.sqrt(2.0 / math.pi)
    return x * (0.5 * (1.0 + jnp.tanh(c * (x + 0.044715 * (x * x * x)))))


def _modulate(xf, g, shift, scale):
    ms = jnp.mean(xf * xf, axis=-1, keepdims=True)
    y = xf * lax.rsqrt(ms + EPS) * g
    return y * (1.0 + scale) + shift


def _mod_spec(mod, d):
    if mod.shape[0] == 1:
        return pl.BlockSpec((None, N_MOD, d), lambda b, i: (0, 0, 0))
    return pl.BlockSpec((None, N_MOD, d), lambda b, i: (b, 0, 0))


def _ada_kernel(cond_ref, w_ref, b_ref, o_ref):
    cnd = cond_ref[...]
    s = cnd * _sigmoid(cnd)
    o_ref[...] = jnp.dot(s, w_ref[...], precision=HIGHEST, preferred_element_type=F32) + b_ref[...]


def _ada(cond, ada_w, ada_b):
    depth, d, n = ada_w.shape
    rows = cond.shape[0]
    tn = 1536
    return pl.pallas_call(
        _ada_kernel,
        grid=(depth, n // tn),
        in_specs=[
            pl.BlockSpec((rows, d), lambda l, j: (0, 0)),
            pl.BlockSpec((None, d, tn), lambda l, j: (l, 0, j)),
            pl.BlockSpec((None, 1, tn), lambda l, j: (l, 0, j)),
        ],
        out_specs=pl.BlockSpec((None, rows, tn), lambda l, j: (l, 0, j)),
        out_shape=jax.ShapeDtypeStruct((depth, rows, n), F32),
        compiler_params=_cparams(("arbitrary", "arbitrary")),
        name="ada",
    )(cond, ada_w, ada_b.reshape(depth, 1, n))


def _inproj_kernel(x_ref, mod_ref, nw_ref, w_ref, g_ref, u_ref):
    d = x_ref.shape[-1]
    h = _modulate(x_ref[...], nw_ref[...], mod_ref[0:1, :], mod_ref[1:2, :])
    r = jnp.dot(h.astype(BF16), w_ref[...], preferred_element_type=F32)
    g_ref[...] = _gelu_tanh(r[:, :d]).astype(BF16)
    u_ref[...] = r[:, d:]


def _inproj(x, mod, norm_w, w_bf):
    b, s, d = x.shape
    tm = min(TM_PROJ, s)
    row = pl.BlockSpec((None, tm, d), lambda bi, i: (bi, i, 0))
    return pl.pallas_call(
        _inproj_kernel,
        grid=(b, s // tm),
        in_specs=[
            row,
            _mod_spec(mod, d),
            pl.BlockSpec((1, d), lambda bi, i: (0, 0)),
            pl.BlockSpec((d, 2 * d), lambda bi, i: (0, 0)),
        ],
        out_specs=[row, row],
        out_shape=[jax.ShapeDtypeStruct((b, s, d), BF16), jax.ShapeDtypeStruct((b, s, d), F32)],
        compiler_params=_cparams(("parallel", "parallel")),
        name="lru_inproj",
    )(x, mod, norm_w.reshape(1, d), w_bf)


def _scan_kernel(u_ref, up_ref, un_ref, h0_ref, cw_ref, cb_ref, wg_ref, bg_ref, lam_ref,
                 o_ref, hfin_ref, ubuf, a_s, b_s, carry):
    tc, d = u_ref.shape
    direction = pl.program_id(1)
    j = pl.program_id(2)
    n = pl.num_programs(2)
    chunk = jnp.where(direction == 0, j, n - 1 - j)

    @pl.when(j == 0)
    def _():
        carry[...] = h0_ref[...]

    ubuf[0:SUBLANES, :] = jnp.where(chunk > 0, up_ref[...], 0.0)
    ubuf[SUBLANES:SUBLANES + tc, :] = u_ref[...]
    ubuf[SUBLANES + tc:2 * SUBLANES + tc, :] = jnp.where(chunk < n - 1, un_ref[...], 0.0)
    left = CONV_W // 2
    uc = cb_ref[...]
    for k in range(CONV_W):
        uc = uc + ubuf[pl.ds(SUBLANES - left + k, tc), :] * cw_ref[k:k + 1, :]

    neg_lam = -lam_ref[...]
    softplus = jnp.maximum(neg_lam, 0.0) + jnp.log1p(jnp.exp(-jnp.abs(neg_lam)))
    decay = -LRU_C * softplus
    ub = uc.astype(BF16)
    blk = d // N_LRU_BLOCKS
    for nb in range(N_LRU_BLOCKS):
        sl = slice(blk * nb, blk * (nb + 1))
        z = jnp.dot(ub[:, sl], wg_ref[nb], preferred_element_type=F32)
        r = _sigmoid(z[:, :blk] + bg_ref[0:1, sl])
        ig = _sigmoid(z[:, blk:] + bg_ref[1:2, sl])
        log_a = decay[:, sl] * r
        a = jnp.exp(log_a)
        a_s[:, sl] = a
        b_s[:, sl] = jnp.sqrt(-jnp.tanh(log_a) * (a * a + 1.0)) * (ig * uc[:, sl])

    row = lax.broadcasted_iota(jnp.int32, (SUBLANES, d), 0)
    n_tiles = tc // SUBLANES

    def tile_scan(r0, h, reverse):
        a = a_s[pl.ds(r0, SUBLANES), :]
        bv = b_s[pl.ds(r0, SUBLANES), :]
        for s in (1, 2, 4):
            if reverse:
                valid = row < SUBLANES - s
                shift = SUBLANES - s
            else:
                valid = row >= s
                shift = s
            a_sh = jnp.where(valid, pltpu.roll(a, shift, 0), 1.0)
            b_sh = jnp.where(valid, pltpu.roll(bv, shift, 0), 0.0)
            bv = a * b_sh + bv
            a = a * a_sh
        hh = a * h + bv
        o_ref[pl.ds(r0, SUBLANES), :] = hh
        return hh[0:1, :] if reverse else hh[SUBLANES - 1:SUBLANES, :]

    @pl.when(direction == 0)
    def _():
        def body(i, h):
            return tile_scan(pl.multiple_of(i * SUBLANES, SUBLANES), h, False)
        carry[...] = lax.fori_loop(0, n_tiles, body, carry[...], unroll=4)

    @pl.when(direction == 1)
    def _():
        def body(i, h):
            return tile_scan(pl.multiple_of((n_tiles - 1 - i) * SUBLANES, SUBLANES), h, True)
        carry[...] = lax.fori_loop(0, n_tiles, body, carry[...], unroll=4)

    hfin_ref[...] = carry[...]


def _lru_scan(u, h0, conv_w, conv_b, wg_bf, bg, lam):
    b, s, d = u.shape
    tc = min(TC_SCAN, s)
    n = s // tc
    per = tc // SUBLANES
    last8 = s // SUBLANES - 1
    blk = d // N_LRU_BLOCKS

    def chunk_of(di, j):
        return j + di * (n - 1 - 2 * j)

    return pl.pallas_call(
        _scan_kernel,
        grid=(b, 2, n),
        in_specs=[
            pl.BlockSpec((None, tc, d), lambda bi, di, j: (bi, chunk_of(di, j), 0)),
            pl.BlockSpec((None, SUBLANES, d),
                         lambda bi, di, j: (bi, jnp.maximum(chunk_of(di, j) * per - 1, 0), 0)),
            pl.BlockSpec((None, SUBLANES, d),
                         lambda bi, di, j: (bi, jnp.minimum((chunk_of(di, j) + 1) * per, last8), 0)),
            pl.BlockSpec((None, None, 1, d), lambda bi, di, j: (di, bi, 0, 0)),
            pl.BlockSpec((CONV_W, d), lambda bi, di, j: (0, 0)),
            pl.BlockSpec((1, d), lambda bi, di, j: (0, 0)),
            pl.BlockSpec((None, N_LRU_BLOCKS, blk, 2 * blk), lambda bi, di, j: (di, 0, 0, 0)),
            pl.BlockSpec((None, 2, d), lambda bi, di, j: (di, 0, 0)),
            pl.BlockSpec((None, 1, d), lambda bi, di, j: (di, 0, 0)),
        ],
        out_specs=[
            pl.BlockSpec((None, None, tc, d), lambda bi, di, j: (di, bi, chunk_of(di, j), 0)),
            pl.BlockSpec((None, None, 1, d), lambda bi, di, j: (di, bi, 0, 0)),
        ],
        out_shape=[jax.ShapeDtypeStruct((2, b, s, d), F32), jax.ShapeDtypeStruct((2, b, 1, d), F32)],
        scratch_shapes=[
            pltpu.VMEM((tc + 2 * SUBLANES, d), F32),
            pltpu.VMEM((tc, d), F32),
            pltpu.VMEM((tc, d), F32),
            pltpu.VMEM((1, d), F32),
        ],
        compiler_params=_cparams(("arbitrary", "arbitrary", "arbitrary")),
        name="lru_scan",
    )(u, u, u, h0, conv_w, conv_b.reshape(1, d), wg_bf, bg, lam.reshape(2, 1, d))


def _lru_out_kernel(g_ref, hf_ref, hb_ref, x_ref, mod_ref, w_ref, o_ref):
    y = hf_ref[...] + hb_ref[...]
    t = (g_ref[...].astype(F32) * y).astype(BF16)
    o = jnp.dot(t, w_ref[...], preferred_element_type=F32)
    o_ref[...] = x_ref[...] + mod_ref[2:3, :] * o


def _lru_out(g, h, x, mod, w_bf):
    b, s, d = x.shape
    tm = min(TM_PROJ, s)
    row = pl.BlockSpec((None, tm, d), lambda bi, i: (bi, i, 0))
    return pl.pallas_call(
        _lru_out_kernel,
        grid=(b, s // tm),
        in_specs=[
            row,
            pl.BlockSpec((None, None, tm, d), lambda bi, i: (0, bi, i, 0)),
            pl.BlockSpec((None, None, tm, d), lambda bi, i: (1, bi, i, 0)),
            row,
            _mod_spec(mod, d),
            pl.BlockSpec((d, d), lambda bi, i: (0, 0)),
        ],
        out_specs=row,
        out_shape=jax.ShapeDtypeStruct((b, s, d), F32),
        compiler_params=_cparams(("parallel", "parallel")),
        name="lru_out",
    )(g, h, h, x, mod, w_bf)


def _attn_out_kernel(a_ref, x_ref, mod_ref, w_ref, o_ref):
    o = jnp.dot(a_ref[...], w_ref[...], preferred_element_type=F32)
    o_ref[...] = x_ref[...] + mod_ref[2:3, :] * o


def _attn_out(a, x, mod, w_bf):
    b, s, d = x.shape
    tm = min(TM_PROJ, s)
    row = pl.BlockSpec((None, tm, d), lambda bi, i: (bi, i, 0))
    return pl.pallas_call(
        _attn_out_kernel,
        grid=(b, s // tm),
        in_specs=[row, row, _mod_spec(mod, d), pl.BlockSpec((d, d), lambda bi, i: (0, 0))],
        out_specs=row,
        out_shape=jax.ShapeDtypeStruct((b, s, d), F32),
        compiler_params=_cparams(("parallel", "parallel")),
        name="attn_out",
    )(a, x, mod, w_bf)


def _first_hit(vals, target):
    hits = []
    found = None
    for v in vals:
        eq = v == target
        if found is None:
            hits.append(eq)
            found = eq
        else:
            hits.append(jnp.logical_and(eq, jnp.logical_not(found)))
            found = jnp.logical_or(found, eq)
    return hits


def _route_kernel(x_ref, mod_ref, nw_ref, rwt_ref, rb_ref, tri_ref, h_ref, meta_ref, stat_ref):
    n_exp = rwt_ref.shape[0]
    tm = x_ref.shape[0]
    per_group = n_exp // N_EXPERT_GROUPS
    h = _modulate(x_ref[...], nw_ref[...], mod_ref[3:4, :], mod_ref[4:5, :])
    h_ref[...] = h.astype(BF16)
    logits = lax.dot_general(rwt_ref[...], h, (((1,), (1,)), ((), ())),
                             precision=HIGHEST, preferred_element_type=F32) + rb_ref[...]
    rows = [logits[e:e + 1, :] for e in range(n_exp)]
    mx = functools.reduce(jnp.maximum, rows)
    ex = [jnp.exp(r - mx) for r in rows]
    z = functools.reduce(lambda p, q: p + q, ex)
    probs = [e_ / z for e_ in ex]
    scores = []
    for g in range(N_EXPERT_GROUPS):
        v0, v1, v2, v3 = probs[per_group * g:per_group * (g + 1)]
        hi01, lo01 = jnp.maximum(v0, v1), jnp.minimum(v0, v1)
        hi23, lo23 = jnp.maximum(v2, v3), jnp.minimum(v2, v3)
        top1 = jnp.maximum(hi01, hi23)
        top2 = jnp.maximum(jnp.minimum(hi01, hi23), jnp.maximum(lo01, lo23))
        scores.append(top1 + top2)
    best = functools.reduce(jnp.maximum, scores)
    sel = _first_hit(scores, best)
    masked = [jnp.where(sel[e // per_group], probs[e], -1.0) for e in range(n_exp)]
    m1 = functools.reduce(jnp.maximum, masked)
    is1 = _first_hit(masked, m1)
    masked2 = [jnp.where(is1[e], -2.0, masked[e]) for e in range(n_exp)]
    m2 = functools.reduce(jnp.maximum, masked2)
    is2 = _first_hit(masked2, m2)
    denom = m1 + m2
    gates = [jnp.where(is1[e], m1, jnp.where(is2[e], m2, 0.0)) / denom for e in range(n_exp)]
    zero = jnp.zeros_like(denom)
    for k in range(per_group):
        meta_ref[k:k + 1, :] = functools.reduce(
            lambda p, q: p + q,
            [jnp.where(sel[g], gates[per_group * g + k], 0.0) for g in range(N_EXPERT_GROUPS)])

    onehot = [jnp.where(sel[g], 1.0, 0.0) for g in range(N_EXPERT_GROUPS)]
    onehot_t = jnp.concatenate(onehot + [zero] * (SUBLANES - N_EXPERT_GROUPS), axis=0).astype(BF16)
    earlier = jnp.dot(onehot_t, tri_ref[...], preferred_element_type=F32)
    start = jnp.zeros((1, 1), F32)
    pos = zero
    starts, blocks = [], []
    for g in range(N_EXPERT_GROUPS):
        count = jnp.sum(onehot[g], axis=1, keepdims=True)
        n_blk = jnp.floor((count + (MOE_BLK - 1)) * (1.0 / MOE_BLK))
        pos = pos + onehot[g] * (start + earlier[g:g + 1, :])
        starts.append(start)
        blocks.append(n_blk)
        start = start + n_blk * MOE_BLK
    meta_ref[per_group:per_group + 1, :] = pos
    meta_ref[per_group + 1:SUBLANES, :] = jnp.zeros((SUBLANES - per_group - 1, tm), F32)
    stat_ref[...] = jnp.concatenate([jnp.broadcast_to(v, (1, LANES)) for v in starts + blocks], axis=0)


def _route(x, mod, norm_w, router_w, router_b):
    b, s, d = x.shape
    n_exp = router_w.shape[1]
    assert n_exp // N_EXPERT_GROUPS == 4 and N_EXPERT_GROUPS == 4
    tm = min(TM_MOE, s)
    n_s = s // tm
    row = pl.BlockSpec((None, tm, d), lambda bi, i: (bi, i, 0))
    t_idx = jnp.arange(tm, dtype=jnp.int32)
    tri = (t_idx[:, None] < t_idx[None, :]).astype(BF16)
    return pl.pallas_call(
        _route_kernel,
        grid=(b, n_s),
        in_specs=[
            row,
            _mod_spec(mod, d),
            pl.BlockSpec((1, d), lambda bi, i: (0, 0)),
            pl.BlockSpec((n_exp, d), lambda bi, i: (0, 0)),
            pl.BlockSpec((n_exp, 1), lambda bi, i: (0, 0)),
            pl.BlockSpec((tm, tm), lambda bi, i: (0, 0)),
        ],
        out_specs=[row,
                   pl.BlockSpec((SUBLANES, tm), lambda bi, i: (0, bi * n_s + i)),
                   pl.BlockSpec((None, SUBLANES, LANES), lambda bi, i: (bi * n_s + i, 0, 0))],
        out_shape=[jax.ShapeDtypeStruct((b, s, d), BF16),
                   jax.ShapeDtypeStruct((SUBLANES, b * s), F32),
                   jax.ShapeDtypeStruct((b * n_s, SUBLANES, LANES), F32)],
        compiler_params=_cparams(("parallel", "parallel")),
        name="moe_route",
    )(x, mod, norm_w.reshape(1, d), router_w.T, router_b.reshape(n_exp, 1), tri)


def _moe_kernel(stat_ref, h_ref, mrow_ref, mcol_ref, gcol_ref, x_ref, mod_ref, w1_ref, w3_ref, w2_ref,
                o_ref, xc, gc, yc):
    tile = pl.program_id(0)
    group = pl.program_id(1)
    n_groups = pl.num_programs(1)
    tm = h_ref.shape[0]
    rows = xc.shape[0]
    per_group = N_EXPERT_GROUPS
    ff = w1_ref.shape[1] // per_group

    @pl.when(group == 0)
    def _():
        pos_row = mrow_ref[per_group:per_group + 1, :].astype(jnp.int32)
        h = h_ref[...]
        gcol = gcol_ref[...]
        for c in range(rows // MOE_CHUNK):
            r_id = lax.broadcasted_iota(jnp.int32, (MOE_CHUNK, tm), 0) + c * MOE_CHUNK
            perm = jnp.where(pos_row == r_id, 1.0, 0.0).astype(BF16)
            sl = slice(c * MOE_CHUNK, (c + 1) * MOE_CHUNK)
            xc[sl, :] = jnp.dot(perm, h, preferred_element_type=F32).astype(BF16)
            gc[sl, :] = jnp.dot(perm, gcol, preferred_element_type=F32)
        yc[...] = jnp.zeros_like(yc)

    start = stat_ref[tile * SUBLANES + group]
    n_blk = stat_ref[tile * SUBLANES + per_group + group]

    def block(j, carry):
        r0 = pl.multiple_of(start + j * MOE_BLK, MOE_BLK)
        xb = xc[pl.ds(r0, MOE_BLK), :]
        a = jnp.dot(xb, w1_ref[...], preferred_element_type=F32)
        c = jnp.dot(xb, w3_ref[...], preferred_element_type=F32)
        g = gc[pl.ds(r0, MOE_BLK), :]
        parts = []
        for k in range(per_group):
            gate = g[:, k:k + 1] + g[:, per_group + k:per_group + k + 1]
            ak = a[:, k * ff:(k + 1) * ff]
            parts.append(((ak * _sigmoid(ak)) * c[:, k * ff:(k + 1) * ff] * gate).astype(BF16))
        t = jnp.concatenate(parts, axis=1)
        yc[pl.ds(r0, MOE_BLK), :] = jnp.dot(t, w2_ref[...], preferred_element_type=F32).astype(BF16)
        return carry

    lax.fori_loop(0, n_blk, block, 0)

    @pl.when(group == n_groups - 1)
    def _():
        y_sorted = yc[...]
        for c in range(tm // MOE_CHUNK):
            sl = slice(c * MOE_CHUNK, (c + 1) * MOE_CHUNK)
            pos_col = mcol_ref[sl, per_group:per_group + 1].astype(jnp.int32)
            c_id = lax.broadcasted_iota(jnp.int32, (MOE_CHUNK, rows), 1)
            perm_t = jnp.where(pos_col == c_id, 1.0, 0.0).astype(BF16)
            y = jnp.dot(perm_t, y_sorted, preferred_element_type=F32)
            o_ref[sl, :] = x_ref[sl, :] + mod_ref[5:6, :] * y


def _moe(h, meta, stat, x, mod, w1g, w3g, w2g):
    b, s, d = x.shape
    n_tok = b * s
    tm = min(TM_MOE, s)
    n_tiles = n_tok // tm
    rows = tm + N_EXPERT_GROUPS * MOE_BLK
    rows = -(-rows // MOE_CHUNK) * MOE_CHUNK
    ffg = w1g.shape[2]
    if mod.shape[0] == 1:
        mod_spec = pl.BlockSpec((None, N_MOD, d), lambda i, g, st: (0, 0, 0))
    else:
        mod_spec = pl.BlockSpec((None, N_MOD, d), lambda i, g, st: ((i * tm) // s, 0, 0))
    meta_col = meta.T
    gates = meta_col[:, :N_EXPERT_GROUPS]
    g_hi = gates.astype(BF16)
    g_lo = (gates - g_hi.astype(F32)).astype(BF16)
    gcol = jnp.concatenate([g_hi, g_lo, jnp.zeros((n_tok, LANES - 2 * N_EXPERT_GROUPS), BF16)], axis=1)
    stat_i = stat[:, :, 0].astype(jnp.int32).reshape(-1)
    once = pl.Buffered(1)

    def row(width, mode=None):
        return pl.BlockSpec((tm, width), lambda i, g, st: (i, 0), pipeline_mode=mode)

    grid_spec = pltpu.PrefetchScalarGridSpec(
        num_scalar_prefetch=1,
        grid=(n_tiles, N_EXPERT_GROUPS),
        in_specs=[
            row(d, once),
            pl.BlockSpec((SUBLANES, tm), lambda i, g, st: (0, i)),
            row(SUBLANES),
            row(LANES),
            row(d, once),
            mod_spec,
            pl.BlockSpec((None, d, ffg), lambda i, g, st: (g, 0, 0)),
            pl.BlockSpec((None, d, ffg), lambda i, g, st: (g, 0, 0)),
            pl.BlockSpec((None, ffg, d), lambda i, g, st: (g, 0, 0)),
        ],
        out_specs=row(d, once),
        scratch_shapes=[pltpu.VMEM((rows, d), BF16), pltpu.VMEM((rows, LANES), F32), pltpu.VMEM((rows, d), BF16)],
    )
    out = pl.pallas_call(
        _moe_kernel,
        grid_spec=grid_spec,
        out_shape=jax.ShapeDtypeStruct((n_tok, d), F32),
        compiler_params=_cparams(("parallel", "arbitrary"), VMEM_LIMIT_MOE),
        name="moe_experts",
    )(stat_i, h.reshape(n_tok, d), meta, meta_col, gcol, x.reshape(n_tok, d), mod, w1g, w3g, w2g)
    return out.reshape(b, s, d)


def _group_expert_weights(w1, w3, w2):
    n_exp, d, ff = w1.shape
    per_group = n_exp // N_EXPERT_GROUPS

    def cat_cols(w):
        w = w.astype(BF16).reshape(N_EXPERT_GROUPS, per_group, d, ff)
        return w.transpose(0, 2, 1, 3).reshape(N_EXPERT_GROUPS, d, per_group * ff)

    return cat_cols(w1), cat_cols(w3), w2.astype(BF16).reshape(N_EXPERT_GROUPS, per_group * ff, d)


def _moe_block(x, mod, norm_w, router_w, router_b, w1g, w3g, w2g):
    shape = x.shape
    if mod.shape[0] == 1:
        x = x.reshape(1, -1, shape[-1])
    h, meta, stat = _route(x, mod, norm_w, router_w, router_b)
    return _moe(h, meta, stat, x, mod, w1g, w3g, w2g).reshape(shape)


def _head_norm(t, w_row, ones_blockdiag):
    sq = t * t
    hi = sq.astype(BF16)
    lo = (sq - hi.astype(F32)).astype(BF16)
    ss = (jnp.dot(hi, ones_blockdiag, preferred_element_type=F32)
          + jnp.dot(lo, ones_blockdiag, preferred_element_type=F32))
    return t * lax.rsqrt(ss * (1.0 / HEAD_DIM) + EPS) * w_row


def _rope(t, cos, sin_next, sin_prev):
    quarter = HEAD_DIM // 4
    outs = []
    for nb in range(t.shape[1] // LANES):
        tb = t[:, LANES * nb:LANES * (nb + 1)]
        outs.append(tb * cos
                    + pltpu.roll(tb, LANES - quarter, 1) * sin_next
                    + pltpu.roll(tb, quarter, 1) * sin_prev)
    return jnp.concatenate(outs, axis=1)


def _qkv_kernel(x_ref, mod_ref, nw_ref, w_ref, qn_ref, kn_ref, ones_ref, cos_ref, sn_ref, sp_ref,
                q_ref, k_ref, v_ref):
    n_q = q_ref.shape[-1]
    n_kv = k_ref.shape[-1]
    h = _modulate(x_ref[...], nw_ref[...], mod_ref[0:1, :], mod_ref[1:2, :])
    r = jnp.dot(h.astype(BF16), w_ref[...], preferred_element_type=F32)
    cos, sn, sp = cos_ref[...], sn_ref[...], sp_ref[...]
    q = _head_norm(r[:, :n_q], qn_ref[...], ones_ref[...])
    q_ref[...] = (_rope(q, cos, sn, sp) * (math.log2(math.e) / math.sqrt(HEAD_DIM))).astype(BF16)
    k = _head_norm(r[:, n_q:n_q + n_kv], kn_ref[...], ones_ref[0:n_kv, 0:n_kv])
    k_ref[...] = _rope(k, cos, sn, sp).astype(BF16)
    v_ref[...] = r[:, n_q + n_kv:].astype(BF16)


def _kv_ctx_kernel(x_ref, mod_ref, nw_ref, w_ref, kn_ref, ones_ref, k_ref, v_ref):
    n_kv = k_ref.shape[-1]
    h = _modulate(x_ref[...], nw_ref[...], mod_ref[0:1, :], mod_ref[1:2, :])
    r = jnp.dot(h.astype(BF16), w_ref[...], preferred_element_type=F32)
    k_ref[...] = _head_norm(r[:, :n_kv], kn_ref[...], ones_ref[...]).astype(BF16)
    v_ref[...] = r[:, n_kv:].astype(BF16)


def _rope_tables(s):
    half = HEAD_DIM // 2
    quarter = half // 2
    t = jnp.arange(s, dtype=jnp.int32)
    pos_r = (t // GRID_W).astype(F32)[:, None]
    pos_c = (t % GRID_W).astype(F32)[:, None]
    lane = jnp.arange(LANES, dtype=jnp.int32)[None, :]
    in_head = lane % HEAD_DIM
    in_half = in_head % half
    freq = (in_half % quarter).astype(F32)
    inv = ROPE_THETA ** (-freq / quarter)
    ang = jnp.where(in_head < half, pos_r, pos_c) * inv
    cos = jnp.cos(ang)
    sin = jnp.sin(ang)
    first = in_half < quarter
    return cos, jnp.where(first, -sin, 0.0), jnp.where(first, 0.0, sin)


def _ones_blockdiag(n):
    i = jnp.arange(n, dtype=jnp.int32)
    return (i[:, None] // HEAD_DIM == i[None, :] // HEAD_DIM).astype(BF16)


def _qkv(x, mod, norm_w, w_bf, qn_w, kn_w):
    b, s, d = x.shape
    n_kv = N_KV_HEADS * HEAD_DIM
    n_q = w_bf.shape[1] - 2 * n_kv
    tm = min(TM_PROJ, s)
    cos, sn, sp = _rope_tables(s)
    tab = pl.BlockSpec((tm, LANES), lambda i, bi: (i, 0))

    def row(w):
        return pl.BlockSpec((None, tm, w), lambda i, bi: (bi, i, 0))

    def const(shape):
        return pl.BlockSpec(shape, lambda i, bi: (0,) * len(shape))

    return pl.pallas_call(
        _qkv_kernel,
        grid=(s // tm, b),
        in_specs=[
            row(d),
            pl.BlockSpec((None, N_MOD, d), lambda i, bi: (bi, 0, 0)),
            const((1, d)),
            const((d, n_q + 2 * n_kv)),
            const((1, n_q)),
            const((1, n_kv)),
            const((n_q, n_q)),
            tab, tab, tab,
        ],
        out_specs=[row(n_q), row(n_kv), row(n_kv)],
        out_shape=[jax.ShapeDtypeStruct((b, s, n_q), BF16),
                   jax.ShapeDtypeStruct((b, s, n_kv), BF16),
                   jax.ShapeDtypeStruct((b, s, n_kv), BF16)],
        compiler_params=_cparams(("parallel", "parallel")),
        name="attn_qkv",
    )(x, mod, norm_w.reshape(1, d), w_bf,
      jnp.tile(qn_w, n_q // HEAD_DIM).reshape(1, n_q), jnp.tile(kn_w, N_KV_HEADS).reshape(1, n_kv),
      _ones_blockdiag(n_q), cos, sn, sp)


def _kv_ctx(ctx, mod, norm_w, w_kv_bf, kn_w):
    b, s, d = ctx.shape
    n_kv = N_KV_HEADS * HEAD_DIM
    tm = min(TM_PROJ, s)

    def row(w):
        return pl.BlockSpec((None, tm, w), lambda bi, i: (bi, i, 0))

    def const(shape):
        return pl.BlockSpec(shape, lambda bi, i: (0,) * len(shape))

    return pl.pallas_call(
        _kv_ctx_kernel,
        grid=(b, s // tm),
        in_specs=[row(d), _mod_spec(mod, d), const((1, d)), const((d, 2 * n_kv)),
                  const((1, n_kv)), const((n_kv, n_kv))],
        out_specs=[row(n_kv), row(n_kv)],
        out_shape=[jax.ShapeDtypeStruct((b, s, n_kv), BF16), jax.ShapeDtypeStruct((b, s, n_kv), BF16)],
        compiler_params=_cparams(("parallel", "parallel")),
        name="attn_kv_ctx",
    )(ctx, mod, norm_w.reshape(1, d), w_kv_bf, jnp.tile(kn_w, N_KV_HEADS).reshape(1, n_kv),
      _ones_blockdiag(n_kv))


def _attn_kernel(q_ref, k_ref, vt_ref, o_ref):
    q = q_ref[...]
    k = k_ref[...]
    vt = vt_ref[...]
    n_keys, tq = k.shape[0], q.shape[0]
    n_heads = q.shape[1] // HEAD_DIM
    outs = []
    for j in range(0, n_heads, 2):
        q2 = jnp.concatenate([q[:, HEAD_DIM * (j + i):HEAD_DIM * (j + i + 1)] for i in range(2)], axis=0)
        st = lax.dot_general(k, q2, (((1,), (1,)), ((), ())), preferred_element_type=F32)
        slabs = st.reshape(n_keys // ATTN_SLAB, ATTN_SLAB, 2 * tq)
        m = jnp.max(jnp.max(slabs, axis=0), axis=0, keepdims=True)
        p = jnp.exp2(st - m).astype(BF16)
        ot = jnp.dot(vt, p, preferred_element_type=F32)
        on = ot[:HEAD_DIM, :] / ot[HEAD_DIM:HEAD_DIM + 1, :]
        outs += [on[:, :tq], on[:, tq:]]
    o_ref[...] = jnp.concatenate(outs, axis=0).T.astype(BF16)


def _attention(q, k, vt):
    assert k.shape[2] % ATTN_SLAB == 0
    b, s, n_q = q.shape
    n_keys = k.shape[2]
    group_w = n_q // N_KV_HEADS
    tq = min(TQ_ATTN, s)
    return pl.pallas_call(
        _attn_kernel,
        grid=(b, N_KV_HEADS, s // tq),
        in_specs=[
            pl.BlockSpec((None, tq, group_w), lambda bi, g, i: (bi, i, g)),
            pl.BlockSpec((None, None, n_keys, HEAD_DIM), lambda bi, g, i: (bi, g, 0, 0)),
            pl.BlockSpec((None, None, vt.shape[2], n_keys), lambda bi, g, i: (bi, g, 0, 0)),
        ],
        out_specs=pl.BlockSpec((None, tq, group_w), lambda bi, g, i: (bi, i, g)),
        out_shape=jax.ShapeDtypeStruct((b, s, n_q), BF16),
        compiler_params=_cparams(("parallel", "parallel", "parallel")),
        name="attn_core",
    )(q, k, vt)


def _modulations(c, c_ctx, ada_w, ada_b):
    b, d = c.shape
    depth = ada_w.shape[0]
    assert b <= SUBLANES
    cond = jnp.zeros((2 * SUBLANES, d), F32).at[:b].set(c).at[b].set(c_ctx)
    mods = _ada(cond, ada_w, ada_b)
    mod_x = [mods[l, :b].reshape(b, N_MOD, d) for l in range(depth)]
    mod_c = [mods[l, b:b + 1].reshape(1, N_MOD, d) for l in range(depth)]
    return mod_x, mod_c


def _lru_layer(x, ctx, mod_x, mod_c, norm_w, in_w, conv_w, conv_b, ga_w, ga_b, gx_w, gx_b, lam, out_w):
    b, _, d = x.shape
    in_bf = in_w.astype(BF16)
    out_bf = out_w.astype(BF16)
    wg = jnp.concatenate([ga_w, gx_w], axis=-1).astype(BF16)
    bg = jnp.stack([ga_b, gx_b], axis=1)
    g_c, u_c = _inproj(ctx, mod_c, norm_w, in_bf)
    g_x, u_x = _inproj(x, mod_x, norm_w, in_bf)
    h_c, h_c_end = _lru_scan(u_c, jnp.zeros((2, b, 1, d), F32), conv_w, conv_b, wg, bg, lam)
    h_x, _ = _lru_scan(u_x, h_c_end, conv_w, conv_b, wg, bg, lam)
    return _lru_out(g_x, h_x, x, mod_x, out_bf), _lru_out(g_c, h_c, ctx, mod_c, out_bf)


def _attn_layer(x, ctx, mod_x, mod_c, norm_w, qkv_w, qn_w, kn_w, o_w):
    b, s, _ = x.shape
    n_kv = N_KV_HEADS * HEAD_DIM
    qkv_bf = qkv_w.astype(BF16)
    n_q = qkv_bf.shape[1] - 2 * n_kv
    q, k_x, v_x = _qkv(x, mod_x, norm_w, qkv_bf, qn_w, kn_w)
    k_c, v_c = _kv_ctx(ctx, mod_c, norm_w, qkv_bf[:, n_q:], kn_w)
    n_keys = ctx.shape[1] + s
    k_all = jnp.concatenate([k_c, k_x], axis=1).reshape(b, n_keys, N_KV_HEADS, HEAD_DIM)
    v_all = jnp.concatenate([v_c, v_x], axis=1).reshape(b, n_keys, N_KV_HEADS, HEAD_DIM)
    ones_row = jnp.zeros((b, N_KV_HEADS, BF16_SUBLANES, n_keys), BF16).at[:, :, 0].set(1.0)
    vt = jnp.concatenate([v_all.transpose(0, 2, 3, 1), ones_row], axis=2)
    att = _attention(q, k_all.transpose(0, 2, 1, 3), vt)
    return _attn_out(att, x, mod_x, o_w.astype(BF16))


def kernel(x, c, ctx, c_ctx, ada_w, ada_b, norm_mix_w, norm_ffn_w, lru_in_w, lru_conv_w, lru_conv_b, lru_gate_a_w, lru_gate_a_b, lru_gate_x_w, lru_gate_x_b, lru_lambda, lru_out_w, attn_qkv_w, attn_q_norm_w, attn_k_norm_w, attn_o_w, router_w, router_b, moe_w1, moe_w3, moe_w2):
    assert ada_w.shape[0] == 2
    mod_x, mod_c = _modulations(c, c_ctx, ada_w, ada_b)

    x, ctx = _lru_layer(x, ctx, mod_x[0], mod_c[0], norm_mix_w[0], lru_in_w[0], lru_conv_w[0], lru_conv_b[0],
                        lru_gate_a_w[0], lru_gate_a_b[0], lru_gate_x_w[0], lru_gate_x_b[0], lru_lambda[0],
                        lru_out_w[0])
    w1, w3, w2 = _group_expert_weights(moe_w1[0], moe_w3[0], moe_w2[0])
    ctx = _moe_block(ctx, mod_c[0], norm_ffn_w[0], router_w, router_b, w1, w3, w2)
    x = _moe_block(x, mod_x[0], norm_ffn_w[0], router_w, router_b, w1, w3, w2)

    x = _attn_layer(x, ctx, mod_x[1], mod_c[1], norm_mix_w[1], attn_qkv_w[0], attn_q_norm_w[0],
                    attn_k_norm_w[0], attn_o_w[0])
    w1, w3, w2 = _group_expert_weights(moe_w1[1], moe_w3[1], moe_w2[1])
    return _moe_block(x, mod_x[1], norm_ffn_w[1], router_w, router_b, w1, w3, w2)
```

```python
import functools
import math

import jax
import jax.numpy as jnp
from jax import lax
from jax.experimental import pallas as pl
from jax.experimental.pallas import tpu as pltpu

F32 = jnp.float32
BF16 = jnp.bfloat16
HIGHEST = lax.Precision.HIGHEST

GRID_W = 64
N_LRU_BLOCKS = 8
CONV_W = 4
LRU_C = 8.0
HEAD_DIM = 64
N_KV_HEADS = 4
ROPE_THETA = 10000.0
N_EXPERT_GROUPS = 4
N_MOD = 6
EPS = 1e-6

LANES = 128
SUBLANES = 8
BF16_SUBLANES = 16
VMEM_LIMIT = 48 * 1024 * 1024
VMEM_LIMIT_MOE = 56 * 1024 * 1024

TM_PROJ = 512
TC_SCAN = 512
TQ_ATTN = 512
ATTN_QSUB = 256
ATTN_SLAB = 256
TM_MOE = 1024
MOE_BLK = 128
MOE_CHUNK = 256


def _cparams(sem, vmem_limit=VMEM_LIMIT):
    return pltpu.CompilerParams(dimension_semantics=sem, vmem_limit_bytes=vmem_limit)


def _sigmoid(x):
    return 1.0 / (1.0 + jnp.exp(-x))


def _gelu_tanh(x):
    c = math.sqrt(2.0 / math.pi)
    return x * (0.5 * (1.0 + jnp.tanh(c * (x + 0.044715 * (x * x * x)))))


def _modulate(xf, g, shift, scale):
    ms = jnp.mean(xf * xf, axis=-1, keepdims=True)
    y = xf * lax.rsqrt(ms + EPS) * g
    return y * (1.0 + scale) + shift


def _mod_spec(mod, d):
    if mod.shape[0] == 1:
        return pl.BlockSpec((None, N_MOD, d), lambda b, i: (0, 0, 0))
    return pl.BlockSpec((None, N_MOD, d), lambda b, i: (b, 0, 0))


def _ada_kernel(cond_ref, w_ref, b_ref, o_ref):
    cnd = cond_ref[...]
    s = cnd * _sigmoid(cnd)
    o_ref[...] = jnp.dot(s, w_ref[...], precision=HIGHEST, preferred_element_type=F32) + b_ref[...]


def _ada(cond, ada_w, ada_b):
    depth, d, n = ada_w.shape
    rows = cond.shape[0]
    tn = 1536
    return pl.pallas_call(
        _ada_kernel,
        grid=(depth, n // tn),
        in_specs=[
            pl.BlockSpec((rows, d), lambda l, j: (0, 0)),
            pl.BlockSpec((None, d, tn), lambda l, j: (l, 0, j)),
            pl.BlockSpec((None, 1, tn), lambda l, j: (l, 0, j)),
        ],
        out_specs=pl.BlockSpec((None, rows, tn), lambda l, j: (l, 0, j)),
        out_shape=jax.ShapeDtypeStruct((depth, rows, n), F32),
        compiler_params=_cparams(("arbitrary", "arbitrary")),
        name="ada",
    )(cond, ada_w, ada_b.reshape(depth, 1, n))


def _inproj_kernel(x_ref, mod_ref, nw_ref, w_ref, g_ref, u_ref):
    d = x_ref.shape[-1]
    h = _modulate(x_ref[...], nw_ref[...], mod_ref[0:1, :], mod_ref[1:2, :])
    r = jnp.dot(h.astype(BF16), w_ref[...], preferred_element_type=F32)
    g_ref[...] = _gelu_tanh(r[:, :d]).astype(BF16)
    u_ref[...] = r[:, d:]


def _inproj(x, mod, norm_w, w_bf):
    b, s, d = x.shape
    tm = min(TM_PROJ, s)
    row = pl.BlockSpec((None, tm, d), lambda bi, i: (bi, i, 0))
    return pl.pallas_call(
        _inproj_kernel,
        grid=(b, s // tm),
        in_specs=[
            row,
            _mod_spec(mod, d),
            pl.BlockSpec((1, d), lambda bi, i: (0, 0)),
            pl.BlockSpec((d, 2 * d), lambda bi, i: (0, 0)),
        ],
        out_specs=[row, row],
        out_shape=[jax.ShapeDtypeStruct((b, s, d), BF16), jax.ShapeDtypeStruct((b, s, d), F32)],
        compiler_params=_cparams(("parallel", "parallel")),
        name="lru_inproj",
    )(x, mod, norm_w.reshape(1, d), w_bf)


def _scan_kernel(u_ref, up_ref, un_ref, h0_ref, cw_ref, cb_ref, wg_ref, bg_ref, lam_ref,
                 o_ref, hfin_ref, ubuf, a_s, b_s, carry):
    tc, d = u_ref.shape
    direction = pl.program_id(1)
    j = pl.program_id(2)
    n = pl.num_programs(2)
    chunk = jnp.where(direction == 0, j, n - 1 - j)

    @pl.when(j == 0)
    def _():
        carry[...] = h0_ref[...]

    ubuf[0:SUBLANES, :] = jnp.where(chunk > 0, up_ref[...], 0.0)
    ubuf[SUBLANES:SUBLANES + tc, :] = u_ref[...]
    ubuf[SUBLANES + tc:2 * SUBLANES + tc, :] = jnp.where(chunk < n - 1, un_ref[...], 0.0)
    left = CONV_W // 2
    uc = cb_ref[...]
    for k in range(CONV_W):
        uc = uc + ubuf[pl.ds(SUBLANES - left + k, tc), :] * cw_ref[k:k + 1, :]

    neg_lam = -lam_ref[...]
    softplus = jnp.maximum(neg_lam, 0.0) + jnp.log1p(jnp.exp(-jnp.abs(neg_lam)))
    decay = -LRU_C * softplus
    ub = uc.astype(BF16)
    blk = d // N_LRU_BLOCKS
    for nb in range(N_LRU_BLOCKS):
        sl = slice(blk * nb, blk * (nb + 1))
        z = jnp.dot(ub[:, sl], wg_ref[nb], preferred_element_type=F32)
        r = _sigmoid(z[:, :blk] + bg_ref[0:1, sl])
        ig = _sigmoid(z[:, blk:] + bg_ref[1:2, sl])
        log_a = decay[:, sl] * r
        a = jnp.exp(log_a)
        a_s[:, sl] = a
        b_s[:, sl] = jnp.sqrt(-jnp.tanh(log_a) * (a * a + 1.0)) * (ig * uc[:, sl])

    row = lax.broadcasted_iota(jnp.int32, (SUBLANES, d), 0)
    n_tiles = tc // SUBLANES

    def tile_scan(r0, h, reverse):
        a = a_s[pl.ds(r0, SUBLANES), :]
        bv = b_s[pl.ds(r0, SUBLANES), :]
        for s in (1, 2, 4):
            if reverse:
                valid = row < SUBLANES - s
                shift = SUBLANES - s
            else:
                valid = row >= s
                shift = s
            a_sh = jnp.where(valid, pltpu.roll(a, shift, 0), 1.0)
            b_sh = jnp.where(valid, pltpu.roll(bv, shift, 0), 0.0)
            bv = a * b_sh + bv
            a = a * a_sh
        hh = a * h + bv
        o_ref[pl.ds(r0, SUBLANES), :] = hh
        return hh[0:1, :] if reverse else hh[SUBLANES - 1:SUBLANES, :]

    @pl.when(direction == 0)
    def _():
        def body(i, h):
            return tile_scan(pl.multiple_of(i * SUBLANES, SUBLANES), h, False)
        carry[...] = lax.fori_loop(0, n_tiles, body, carry[...], unroll=4)

    @pl.when(direction == 1)
    def _():
        def body(i, h):
            return tile_scan(pl.multiple_of((n_tiles - 1 - i) * SUBLANES, SUBLANES), h, True)
        carry[...] = lax.fori_loop(0, n_tiles, body, carry[...], unroll=4)

    hfin_ref[...] = carry[...]


def _lru_scan(u, h0, conv_w, conv_b, wg_bf, bg, lam):
    b, s, d = u.shape
    tc = min(TC_SCAN, s)
    n = s // tc
    per = tc // SUBLANES
    last8 = s // SUBLANES - 1
    blk = d // N_LRU_BLOCKS

    def chunk_of(di, j):
        return j + di * (n - 1 - 2 * j)

    return pl.pallas_call(
        _scan_kernel,
        grid=(b, 2, n),
        in_specs=[
            pl.BlockSpec((None, tc, d), lambda bi, di, j: (bi, chunk_of(di, j), 0)),
            pl.BlockSpec((None, SUBLANES, d),
                         lambda bi, di, j: (bi, jnp.maximum(chunk_of(di, j) * per - 1, 0), 0)),
            pl.BlockSpec((None, SUBLANES, d),
                         lambda bi, di, j: (bi, jnp.minimum((chunk_of(di, j) + 1) * per, last8), 0)),
            pl.BlockSpec((None, None, 1, d), lambda bi, di, j: (di, bi, 0, 0)),
            pl.BlockSpec((CONV_W, d), lambda bi, di, j: (0, 0)),
            pl.BlockSpec((1, d), lambda bi, di, j: (0, 0)),
            pl.BlockSpec((None, N_LRU_BLOCKS, blk, 2 * blk), lambda bi, di, j: (di, 0, 0, 0)),
            pl.BlockSpec((None, 2, d), lambda bi, di, j: (di, 0, 0)),
            pl.BlockSpec((None, 1, d), lambda bi, di, j: (di, 0, 0)),
        ],
        out_specs=[
            pl.BlockSpec((None, None, tc, d), lambda bi, di, j: (di, bi, chunk_of(di, j), 0)),
            pl.BlockSpec((None, None, 1, d), lambda bi, di, j: (di, bi, 0, 0)),
        ],
        out_shape=[jax.ShapeDtypeStruct((2, b, s, d), F32), jax.ShapeDtypeStruct((2, b, 1, d), F32)],
        scratch_shapes=[
            pltpu.VMEM((tc + 2 * SUBLANES, d), F32),
            pltpu.VMEM((tc, d), F32),
            pltpu.VMEM((tc, d), F32),
            pltpu.VMEM((1, d), F32),
        ],
        compiler_params=_cparams(("arbitrary", "arbitrary", "arbitrary")),
        name="lru_scan",
    )(u, u, u, h0, conv_w, conv_b.reshape(1, d), wg_bf, bg, lam.reshape(2, 1, d))


def _lru_out_kernel(g_ref, hf_ref, hb_ref, x_ref, mod_ref, w_ref, o_ref):
    y = hf_ref[...] + hb_ref[...]
    t = (g_ref[...].astype(F32) * y).astype(BF16)
    o = jnp.dot(t, w_ref[...], preferred_element_type=F32)
    o_ref[...] = x_ref[...] + mod_ref[2:3, :] * o


def _lru_out(g, h, x, mod, w_bf):
    b, s, d = x.shape
    tm = min(TM_PROJ, s)
    row = pl.BlockSpec((None, tm, d), lambda bi, i: (bi, i, 0))
    return pl.pallas_call(
        _lru_out_kernel,
        grid=(b, s // tm),
        in_specs=[
            row,
            pl.BlockSpec((None, None, tm, d), lambda bi, i: (0, bi, i, 0)),
            pl.BlockSpec((None, None, tm, d), lambda bi, i: (1, bi, i, 0)),
            row,
            _mod_spec(mod, d),
            pl.BlockSpec((d, d), lambda bi, i: (0, 0)),
        ],
        out_specs=row,
        out_shape=jax.ShapeDtypeStruct((b, s, d), F32),
        compiler_params=_cparams(("parallel", "parallel")),
        name="lru_out",
    )(g, h, h, x, mod, w_bf)


def _attn_out_kernel(a_ref, x_ref, mod_ref, w_ref, o_ref):
    o = jnp.dot(a_ref[...], w_ref[...], preferred_element_type=F32)
    o_ref[...] = x_ref[...] + mod_ref[2:3, :] * o


def _attn_out(a, x, mod, w_bf):
    b, s, d = x.shape
    tm = min(TM_PROJ, s)
    row = pl.BlockSpec((None, tm, d), lambda bi, i: (bi, i, 0))
    return pl.pallas_call(
        _attn_out_kernel,
        grid=(b, s // tm),
        in_specs=[row, row, _mod_spec(mod, d), pl.BlockSpec((d, d), lambda bi, i: (0, 0))],
        out_specs=row,
        out_shape=jax.ShapeDtypeStruct((b, s, d), F32),
        compiler_params=_cparams(("parallel", "parallel")),
        name="attn_out",
    )(a, x, mod, w_bf)


def _first_hit(vals, target):
    hits = []
    found = None
    for v in vals:
        eq = v == target
        if found is None:
            hits.append(eq)
            found = eq
        else:
            hits.append(jnp.logical_and(eq, jnp.logical_not(found)))
            found = jnp.logical_or(found, eq)
    return hits


def _route_kernel(x_ref, mod_ref, nw_ref, rwt_ref, rb_ref, tri_ref, h_ref, meta_ref, stat_ref):
    n_exp = rwt_ref.shape[0]
    tm = x_ref.shape[0]
    per_group = n_exp // N_EXPERT_GROUPS
    h = _modulate(x_ref[...], nw_ref[...], mod_ref[3:4, :], mod_ref[4:5, :])
    h_ref[...] = h.astype(BF16)
    logits = lax.dot_general(rwt_ref[...], h, (((1,), (1,)), ((), ())),
                             precision=HIGHEST, preferred_element_type=F32) + rb_ref[...]
    rows = [logits[e:e + 1, :] for e in range(n_exp)]
    mx = functools.reduce(jnp.maximum, rows)
    ex = [jnp.exp(r - mx) for r in rows]
    z = functools.reduce(lambda p, q: p + q, ex)
    probs = [e_ / z for e_ in ex]
    scores = []
    for g in range(N_EXPERT_GROUPS):
        v0, v1, v2, v3 = probs[per_group * g:per_group * (g + 1)]
        hi01, lo01 = jnp.maximum(v0, v1), jnp.minimum(v0, v1)
        hi23, lo23 = jnp.maximum(v2, v3), jnp.minimum(v2, v3)
        top1 = jnp.maximum(hi01, hi23)
        top2 = jnp.maximum(jnp.minimum(hi01, hi23), jnp.maximum(lo01, lo23))
        scores.append(top1 + top2)
    best = functools.reduce(jnp.maximum, scores)
    sel = _first_hit(scores, best)
    masked = [jnp.where(sel[e // per_group], probs[e], -1.0) for e in range(n_exp)]
    m1 = functools.reduce(jnp.maximum, masked)
    is1 = _first_hit(masked, m1)
    masked2 = [jnp.where(is1[e], -2.0, masked[e]) for e in range(n_exp)]
    m2 = functools.reduce(jnp.maximum, masked2)
    is2 = _first_hit(masked2, m2)
    denom = m1 + m2
    gates = [jnp.where(is1[e], m1, jnp.where(is2[e], m2, 0.0)) / denom for e in range(n_exp)]
    zero = jnp.zeros_like(denom)
    for k in range(per_group):
        meta_ref[k:k + 1, :] = functools.reduce(
            lambda p, q: p + q,
            [jnp.where(sel[g], gates[per_group * g + k], 0.0) for g in range(N_EXPERT_GROUPS)])

    onehot = [jnp.where(sel[g], 1.0, 0.0) for g in range(N_EXPERT_GROUPS)]
    onehot_t = jnp.concatenate(onehot + [zero] * (SUBLANES - N_EXPERT_GROUPS), axis=0).astype(BF16)
    earlier = jnp.dot(onehot_t, tri_ref[...], preferred_element_type=F32)
    start = jnp.zeros((1, 1), F32)
    pos = zero
    starts, blocks = [], []
    for g in range(N_EXPERT_GROUPS):
        count = jnp.sum(onehot[g], axis=1, keepdims=True)
        n_blk = jnp.floor((count + (MOE_BLK - 1)) * (1.0 / MOE_BLK))
        pos = pos + onehot[g] * (start + earlier[g:g + 1, :])
        starts.append(start)
        blocks.append(n_blk)
        start = start + n_blk * MOE_BLK
    meta_ref[per_group:per_group + 1, :] = pos
    meta_ref[per_group + 1:SUBLANES, :] = jnp.zeros((SUBLANES - per_group - 1, tm), F32)
    stat_ref[...] = jnp.concatenate([jnp.broadcast_to(v, (1, LANES)) for v in starts + blocks], axis=0)


def _route(x, mod, norm_w, router_w, router_b):
    b, s, d = x.shape
    n_exp = router_w.shape[1]
    assert n_exp // N_EXPERT_GROUPS == 4 and N_EXPERT_GROUPS == 4
    tm = min(TM_MOE, s)
    n_s = s // tm
    row = pl.BlockSpec((None, tm, d), lambda bi, i: (bi, i, 0))
    t_idx = jnp.arange(tm, dtype=jnp.int32)
    tri = (t_idx[:, None] < t_idx[None, :]).astype(BF16)
    return pl.pallas_call(
        _route_kernel,
        grid=(b, n_s),
        in_specs=[
            row,
            _mod_spec(mod, d),
            pl.BlockSpec((1, d), lambda bi, i: (0, 0)),
            pl.BlockSpec((n_exp, d), lambda bi, i: (0, 0)),
            pl.BlockSpec((n_exp, 1), lambda bi, i: (0, 0)),
            pl.BlockSpec((tm, tm), lambda bi, i: (0, 0)),
        ],
        out_specs=[row,
                   pl.BlockSpec((SUBLANES, tm), lambda bi, i: (0, bi * n_s + i)),
                   pl.BlockSpec((None, SUBLANES, LANES), lambda bi, i: (bi * n_s + i, 0, 0))],
        out_shape=[jax.ShapeDtypeStruct((b, s, d), BF16),
                   jax.ShapeDtypeStruct((SUBLANES, b * s), F32),
                   jax.ShapeDtypeStruct((b * n_s, SUBLANES, LANES), F32)],
        compiler_params=_cparams(("parallel", "parallel")),
        name="moe_route",
    )(x, mod, norm_w.reshape(1, d), router_w.T, router_b.reshape(n_exp, 1), tri)


def _moe_kernel(stat_ref, h_ref, mrow_ref, mcol_ref, gcol_ref, x_ref, mod_ref, w1_ref, w3_ref, w2_ref,
                o_ref, xc, gc, yc):
    pair = pl.program_id(0)
    step = pl.program_id(1)
    tm = h_ref.shape[0]
    rows = xc.shape[1]
    per_group = N_EXPERT_GROUPS
    n_gather, n_expert = 2, 2 * N_EXPERT_GROUPS

    for slot in range(2):
        @pl.when(step == slot)
        def _(slot=slot):
            pos_row = mrow_ref[per_group:per_group + 1, :].astype(jnp.int32)
            h = h_ref[...]
            gcol = gcol_ref[...]
            for c in range(rows // MOE_CHUNK):
                r_id = lax.broadcasted_iota(jnp.int32, (MOE_CHUNK, tm), 0) + c * MOE_CHUNK
                perm = jnp.where(pos_row == r_id, 1.0, 0.0).astype(BF16)
                sl = slice(c * MOE_CHUNK, (c + 1) * MOE_CHUNK)
                xc[slot, sl, :] = jnp.dot(perm, h, preferred_element_type=F32).astype(BF16)
                gc[slot, sl, :] = jnp.dot(perm, gcol, preferred_element_type=F32)
            yc[slot] = jnp.zeros((rows, yc.shape[2]), BF16)

    @pl.when(jnp.logical_and(step >= n_gather, step < n_gather + n_expert))
    def _():
        e_step = step - n_gather
        group = e_step // 2
        slot = ((e_step + 1) // 2) % 2
        base = (2 * pair + slot) * SUBLANES
        start = stat_ref[base + group]
        n_blk = stat_ref[base + per_group + group]

        def block(j, carry):
            r0 = pl.multiple_of(start + j * MOE_BLK, MOE_BLK)
            xb = xc[slot, pl.ds(r0, MOE_BLK), :]
            g = gc[slot, pl.ds(r0, MOE_BLK), :]
            parts = []
            for k in range(per_group):
                a = jnp.dot(xb, w1_ref[k], preferred_element_type=F32)
                c = jnp.dot(xb, w3_ref[k], preferred_element_type=F32)
                gate = g[:, k:k + 1] + g[:, per_group + k:per_group + k + 1]
                parts.append(((a * _sigmoid(a)) * c * gate).astype(BF16))
            t = jnp.concatenate(parts, axis=1)
            yc[slot, pl.ds(r0, MOE_BLK), :] = jnp.dot(t, w2_ref[...], preferred_element_type=F32).astype(BF16)
            return carry

        lax.fori_loop(0, n_blk, block, 0)

    for slot in range(2):
        @pl.when(step == n_gather + n_expert + slot)
        def _(slot=slot):
            y_sorted = yc[slot]
            for c in range(tm // MOE_CHUNK):
                sl = slice(c * MOE_CHUNK, (c + 1) * MOE_CHUNK)
                pos_col = mcol_ref[sl, per_group:per_group + 1].astype(jnp.int32)
                c_id = lax.broadcasted_iota(jnp.int32, (MOE_CHUNK, rows), 1)
                perm_t = jnp.where(pos_col == c_id, 1.0, 0.0).astype(BF16)
                y = jnp.dot(perm_t, y_sorted, preferred_element_type=F32)
                o_ref[sl, :] = x_ref[sl, :] + mod_ref[5:6, :] * y


def _moe(h, meta, stat, x, mod, w1_bf, w3_bf, w2g):
    b, s, d = x.shape
    n_tok = b * s
    tm = min(TM_MOE, s)
    n_tiles = n_tok // tm
    assert n_tiles % 2 == 0
    rows = tm + N_EXPERT_GROUPS * MOE_BLK
    rows = -(-rows // MOE_CHUNK) * MOE_CHUNK
    per_group, ff = w1_bf.shape[0] // N_EXPERT_GROUPS, w1_bf.shape[2]
    n_gather, n_expert = 2, 2 * N_EXPERT_GROUPS
    n_steps = n_gather + n_expert + 2

    def gather_tile(p, t):
        return 2 * p + jnp.minimum(t, 1)

    def scatter_tile(p, t):
        return 2 * p + jnp.where(t == n_steps - 1, 1, 0)

    def group_of(t):
        return jnp.clip((t - n_gather) // 2, 0, N_EXPERT_GROUPS - 1)

    if mod.shape[0] == 1:
        mod_spec = pl.BlockSpec((None, N_MOD, d), lambda p, t, st: (0, 0, 0))
    else:
        mod_spec = pl.BlockSpec((None, N_MOD, d), lambda p, t, st: ((scatter_tile(p, t) * tm) // s, 0, 0))
    meta_col = meta.T
    gates = meta_col[:, :N_EXPERT_GROUPS]
    g_hi = gates.astype(BF16)
    g_lo = (gates - g_hi.astype(F32)).astype(BF16)
    gcol = jnp.concatenate([g_hi, g_lo, jnp.zeros((n_tok, LANES - 2 * N_EXPERT_GROUPS), BF16)], axis=1)
    stat_i = stat[:, :, 0].astype(jnp.int32).reshape(-1)
    once = pl.Buffered(1)

    grid_spec = pltpu.PrefetchScalarGridSpec(
        num_scalar_prefetch=1,
        grid=(n_tiles // 2, n_steps),
        in_specs=[
            pl.BlockSpec((tm, d), lambda p, t, st: (gather_tile(p, t), 0), pipeline_mode=once),
            pl.BlockSpec((SUBLANES, tm), lambda p, t, st: (0, gather_tile(p, t))),
            pl.BlockSpec((tm, SUBLANES), lambda p, t, st: (scatter_tile(p, t), 0)),
            pl.BlockSpec((tm, LANES), lambda p, t, st: (gather_tile(p, t), 0)),
            pl.BlockSpec((tm, d), lambda p, t, st: (scatter_tile(p, t), 0), pipeline_mode=once),
            mod_spec,
            pl.BlockSpec((per_group, d, ff), lambda p, t, st: (group_of(t), 0, 0)),
            pl.BlockSpec((per_group, d, ff), lambda p, t, st: (group_of(t), 0, 0)),
            pl.BlockSpec((None, per_group * ff, d), lambda p, t, st: (group_of(t), 0, 0)),
        ],
        out_specs=pl.BlockSpec((tm, d), lambda p, t, st: (scatter_tile(p, t), 0), pipeline_mode=once),
        scratch_shapes=[pltpu.VMEM((2, rows, d), BF16), pltpu.VMEM((2, rows, LANES), F32),
                        pltpu.VMEM((2, rows, d), BF16)],
    )
    out = pl.pallas_call(
        _moe_kernel,
        grid_spec=grid_spec,
        out_shape=jax.ShapeDtypeStruct((n_tok, d), F32),
        compiler_params=_cparams(("parallel", "arbitrary"), VMEM_LIMIT_MOE),
        name="moe_experts",
    )(stat_i, h.reshape(n_tok, d), meta, meta_col, gcol, x.reshape(n_tok, d), mod, w1_bf, w3_bf, w2g)
    return out.reshape(b, s, d)


def _group_expert_weights(w1, w3, w2):
    d = w2.shape[2]
    return w1.astype(BF16), w3.astype(BF16), w2.astype(BF16).reshape(N_EXPERT_GROUPS, -1, d)


def _moe_block(x, mod, norm_w, router_w, router_b, w1_bf, w3_bf, w2g):
    shape = x.shape
    if mod.shape[0] == 1:
        x = x.reshape(1, -1, shape[-1])
    h, meta, stat = _route(x, mod, norm_w, router_w, router_b)
    return _moe(h, meta, stat, x, mod, w1_bf, w3_bf, w2g).reshape(shape)


def _head_norm(t, w_row, ones_blockdiag):
    sq = t * t
    hi = sq.astype(BF16)
    lo = (sq - hi.astype(F32)).astype(BF16)
    ss = (jnp.dot(hi, ones_blockdiag, preferred_element_type=F32)
          + jnp.dot(lo, ones_blockdiag, preferred_element_type=F32))
    return t * lax.rsqrt(ss * (1.0 / HEAD_DIM) + EPS) * w_row


def _rope(t, cos, sin_next, sin_prev):
    quarter = HEAD_DIM // 4
    outs = []
    for nb in range(t.shape[1] // LANES):
        tb = t[:, LANES * nb:LANES * (nb + 1)]
        outs.append(tb * cos
                    + pltpu.roll(tb, LANES - quarter, 1) * sin_next
                    + pltpu.roll(tb, quarter, 1) * sin_prev)
    return jnp.concatenate(outs, axis=1)


def _qkv_kernel(x_ref, mod_ref, nw_ref, w_ref, qn_ref, kn_ref, ones_ref, cos_ref, sn_ref, sp_ref,
                q_ref, k_ref, v_ref):
    n_q = q_ref.shape[-1]
    n_kv = k_ref.shape[-1]
    h = _modulate(x_ref[...], nw_ref[...], mod_ref[0:1, :], mod_ref[1:2, :])
    r = jnp.dot(h.astype(BF16), w_ref[...], preferred_element_type=F32)
    cos, sn, sp = cos_ref[...], sn_ref[...], sp_ref[...]
    q = _head_norm(r[:, :n_q], qn_ref[...], ones_ref[...])
    q_ref[...] = (_rope(q, cos, sn, sp) * (math.log2(math.e) / math.sqrt(HEAD_DIM))).astype(BF16)
    k = _head_norm(r[:, n_q:n_q + n_kv], kn_ref[...], ones_ref[0:n_kv, 0:n_kv])
    k_ref[...] = _rope(k, cos, sn, sp).astype(BF16)
    v_ref[...] = r[:, n_q + n_kv:].astype(BF16)


def _kv_ctx_kernel(x_ref, mod_ref, nw_ref, w_ref, kn_ref, ones_ref, k_ref, v_ref):
    n_kv = k_ref.shape[-1]
    h = _modulate(x_ref[...], nw_ref[...], mod_ref[0:1, :], mod_ref[1:2, :])
    r = jnp.dot(h.astype(BF16), w_ref[...], preferred_element_type=F32)
    k_ref[...] = _head_norm(r[:, :n_kv], kn_ref[...], ones_ref[...]).astype(BF16)
    v_ref[...] = r[:, n_kv:].astype(BF16)


def _rope_tables(s):
    half = HEAD_DIM // 2
    quarter = half // 2
    t = jnp.arange(s, dtype=jnp.int32)
    pos_r = (t // GRID_W).astype(F32)[:, None]
    pos_c = (t % GRID_W).astype(F32)[:, None]
    lane = jnp.arange(LANES, dtype=jnp.int32)[None, :]
    in_head = lane % HEAD_DIM
    in_half = in_head % half
    freq = (in_half % quarter).astype(F32)
    inv = ROPE_THETA ** (-freq / quarter)
    ang = jnp.where(in_head < half, pos_r, pos_c) * inv
    cos = jnp.cos(ang)
    sin = jnp.sin(ang)
    first = in_half < quarter
    return cos, jnp.where(first, -sin, 0.0), jnp.where(first, 0.0, sin)


def _ones_blockdiag(n):
    i = jnp.arange(n, dtype=jnp.int32)
    return (i[:, None] // HEAD_DIM == i[None, :] // HEAD_DIM).astype(BF16)


def _qkv(x, mod, norm_w, w_bf, qn_w, kn_w):
    b, s, d = x.shape
    n_kv = N_KV_HEADS * HEAD_DIM
    n_q = w_bf.shape[1] - 2 * n_kv
    tm = min(TM_PROJ, s)
    cos, sn, sp = _rope_tables(s)
    tab = pl.BlockSpec((tm, LANES), lambda i, bi: (i, 0))

    def row(w):
        return pl.BlockSpec((None, tm, w), lambda i, bi: (bi, i, 0))

    def const(shape):
        return pl.BlockSpec(shape, lambda i, bi: (0,) * len(shape))

    return pl.pallas_call(
        _qkv_kernel,
        grid=(s // tm, b),
        in_specs=[
            row(d),
            pl.BlockSpec((None, N_MOD, d), lambda i, bi: (bi, 0, 0)),
            const((1, d)),
            const((d, n_q + 2 * n_kv)),
            const((1, n_q)),
            const((1, n_kv)),
            const((n_q, n_q)),
            tab, tab, tab,
        ],
        out_specs=[row(n_q), row(n_kv), row(n_kv)],
        out_shape=[jax.ShapeDtypeStruct((b, s, n_q), BF16),
                   jax.ShapeDtypeStruct((b, s, n_kv), BF16),
                   jax.ShapeDtypeStruct((b, s, n_kv), BF16)],
        compiler_params=_cparams(("parallel", "parallel")),
        name="attn_qkv",
    )(x, mod, norm_w.reshape(1, d), w_bf,
      jnp.tile(qn_w, n_q // HEAD_DIM).reshape(1, n_q), jnp.tile(kn_w, N_KV_HEADS).reshape(1, n_kv),
      _ones_blockdiag(n_q), cos, sn, sp)


def _kv_ctx(ctx, mod, norm_w, w_kv_bf, kn_w):
    b, s, d = ctx.shape
    n_kv = N_KV_HEADS * HEAD_DIM
    tm = min(TM_PROJ, s)

    def row(w):
        return pl.BlockSpec((None, tm, w), lambda bi, i: (bi, i, 0))

    def const(shape):
        return pl.BlockSpec(shape, lambda bi, i: (0,) * len(shape))

    return pl.pallas_call(
        _kv_ctx_kernel,
        grid=(b, s // tm),
        in_specs=[row(d), _mod_spec(mod, d), const((1, d)), const((d, 2 * n_kv)),
                  const((1, n_kv)), const((n_kv, n_kv))],
        out_specs=[row(n_kv), row(n_kv)],
        out_shape=[jax.ShapeDtypeStruct((b, s, n_kv), BF16), jax.ShapeDtypeStruct((b, s, n_kv), BF16)],
        compiler_params=_cparams(("parallel", "parallel")),
        name="attn_kv_ctx",
    )(ctx, mod, norm_w.reshape(1, d), w_kv_bf, jnp.tile(kn_w, N_KV_HEADS).reshape(1, n_kv),
      _ones_blockdiag(n_kv))


def _attn_kernel(q_ref, k_ref, vt_ref, o_ref, s_scr):
    q = q_ref[...]
    k = k_ref[...]
    vt = vt_ref[...]
    n_keys = k.shape[0]
    n_head_pairs = q.shape[1] // (2 * HEAD_DIM)
    n_units = n_head_pairs * (q.shape[0] // ATTN_QSUB)

    def scores(u):
        rows = slice(ATTN_QSUB * (u // n_head_pairs), ATTN_QSUB * (u // n_head_pairs + 1))
        h0 = 2 * (u % n_head_pairs)
        q2 = jnp.concatenate([q[rows, HEAD_DIM * (h0 + i):HEAD_DIM * (h0 + i + 1)] for i in range(2)], axis=0)
        st = lax.dot_general(k, q2, (((1,), (1,)), ((), ())), preferred_element_type=F32)
        s_scr[u % 2] = st
        slabs = st.reshape(n_keys // ATTN_SLAB, ATTN_SLAB, 2 * ATTN_QSUB)
        return jnp.max(jnp.max(slabs, axis=0), axis=0, keepdims=True)

    def weighted_values(u, m):
        p = jnp.exp2(s_scr[u % 2] - m).astype(BF16)
        ot = jnp.dot(vt, p, preferred_element_type=F32)
        on = ot[:HEAD_DIM, :] / ot[HEAD_DIM:HEAD_DIM + 1, :]
        return [on[:, :ATTN_QSUB], on[:, ATTN_QSUB:]]

    outs = []
    m_next = scores(0)
    for u in range(n_units):
        m = m_next
        if u + 1 < n_units:
            m_next = scores(u + 1)
        outs += weighted_values(u, m)
        if (u + 1) % n_head_pairs == 0:
            rows = slice(ATTN_QSUB * (u // n_head_pairs), ATTN_QSUB * (u // n_head_pairs + 1))
            o_ref[rows, :] = jnp.concatenate(outs, axis=0).T.astype(BF16)
            outs = []


def _attention(q, k, vt):
    assert k.shape[2] % ATTN_SLAB == 0
    b, s, n_q = q.shape
    n_keys = k.shape[2]
    group_w = n_q // N_KV_HEADS
    tq = min(TQ_ATTN, s)
    return pl.pallas_call(
        _attn_kernel,
        grid=(b, N_KV_HEADS, s // tq),
        in_specs=[
            pl.BlockSpec((None, tq, group_w), lambda bi, g, i: (bi, i, g)),
            pl.BlockSpec((None, None, n_keys, HEAD_DIM), lambda bi, g, i: (bi, g, 0, 0)),
            pl.BlockSpec((None, None, vt.shape[2], n_keys), lambda bi, g, i: (bi, g, 0, 0)),
        ],
        out_specs=pl.BlockSpec((None, tq, group_w), lambda bi, g, i: (bi, i, g)),
        out_shape=jax.ShapeDtypeStruct((b, s, n_q), BF16),
        scratch_shapes=[pltpu.VMEM((2, n_keys, 2 * ATTN_QSUB), F32)],
        compiler_params=_cparams(("parallel", "parallel", "parallel")),
        name="attn_core",
    )(q, k, vt)


def _modulations(c, c_ctx, ada_w, ada_b):
    b, d = c.shape
    depth = ada_w.shape[0]
    assert b <= SUBLANES
    cond = jnp.zeros((2 * SUBLANES, d), F32).at[:b].set(c).at[b].set(c_ctx)
    mods = _ada(cond, ada_w, ada_b)
    mod_x = [mods[l, :b].reshape(b, N_MOD, d) for l in range(depth)]
    mod_c = [mods[l, b:b + 1].reshape(1, N_MOD, d) for l in range(depth)]
    return mod_x, mod_c


def _lru_layer(x, ctx, mod_x, mod_c, norm_w, in_w, conv_w, conv_b, ga_w, ga_b, gx_w, gx_b, lam, out_w):
    b, _, d = x.shape
    in_bf = in_w.astype(BF16)
    out_bf = out_w.astype(BF16)
    wg = jnp.concatenate([ga_w, gx_w], axis=-1).astype(BF16)
    bg = jnp.stack([ga_b, gx_b], axis=1)
    g_c, u_c = _inproj(ctx, mod_c, norm_w, in_bf)
    g_x, u_x = _inproj(x, mod_x, norm_w, in_bf)
    h_c, h_c_end = _lru_scan(u_c, jnp.zeros((2, b, 1, d), F32), conv_w, conv_b, wg, bg, lam)
    h_x, _ = _lru_scan(u_x, h_c_end, conv_w, conv_b, wg, bg, lam)
    return _lru_out(g_x, h_x, x, mod_x, out_bf), _lru_out(g_c, h_c, ctx, mod_c, out_bf)


def _attn_layer(x, ctx, mod_x, mod_c, norm_w, qkv_w, qn_w, kn_w, o_w):
    b, s, _ = x.shape
    n_kv = N_KV_HEADS * HEAD_DIM
    qkv_bf = qkv_w.astype(BF16)
    n_q = qkv_bf.shape[1] - 2 * n_kv
    q, k_x, v_x = _qkv(x, mod_x, norm_w, qkv_bf, qn_w, kn_w)
    k_c, v_c = _kv_ctx(ctx, mod_c, norm_w, qkv_bf[:, n_q:], kn_w)
    n_keys = ctx.shape[1] + s
    k_all = jnp.concatenate([k_c, k_x], axis=1).reshape(b, n_keys, N_KV_HEADS, HEAD_DIM)
    v_all = jnp.concatenate([v_c, v_x], axis=1).reshape(b, n_keys, N_KV_HEADS, HEAD_DIM)
    ones_row = jnp.zeros((b, N_KV_HEADS, BF16_SUBLANES, n_keys), BF16).at[:, :, 0].set(1.0)
    vt = jnp.concatenate([v_all.transpose(0, 2, 3, 1), ones_row], axis=2)
    att = _attention(q, k_all.transpose(0, 2, 1, 3), vt)
    return _attn_out(att, x, mod_x, o_w.astype(BF16))


def kernel(x, c, ctx, c_ctx, ada_w, ada_b, norm_mix_w, norm_ffn_w, lru_in_w, lru_conv_w, lru_conv_b, lru_gate_a_w, lru_gate_a_b, lru_gate_x_w, lru_gate_x_b, lru_lambda, lru_out_w, attn_qkv_w, attn_q_norm_w, attn_k_norm_w, attn_o_w, router_w, router_b, moe_w1, moe_w3, moe_w2):
    assert ada_w.shape[0] == 2
    mod_x, mod_c = _modulations(c, c_ctx, ada_w, ada_b)

    x, ctx = _lru_layer(x, ctx, mod_x[0], mod_c[0], norm_mix_w[0], lru_in_w[0], lru_conv_w[0], lru_conv_b[0],
                        lru_gate_a_w[0], lru_gate_a_b[0], lru_gate_x_w[0], lru_gate_x_b[0], lru_lambda[0],
                        lru_out_w[0])
    w1, w3, w2 = _group_expert_weights(moe_w1[0], moe_w3[0], moe_w2[0])
    ctx = _moe_block(ctx, mod_c[0], norm_ffn_w[0], router_w, router_b, w1, w3, w2)
    x = _moe_block(x, mod_x[0], norm_ffn_w[0], router_w, router_b, w1, w3, w2)

    x = _attn_layer(x, ctx, mod_x[1], mod_c[1], norm_mix_w[1], attn_qkv_w[0], attn_q_norm_w[0],
                    attn_k_norm_w[0], attn_o_w[0])
    w1, w3, w2 = _group_expert_weights(moe_w1[1], moe_w3[1], moe_w2[1])
    return _moe_block(x, mod_x[1], norm_ffn_w[1], router_w, router_b, w1, w3, w2)
```

```python
import functools
import math

import jax
import jax.numpy as jnp
from jax import lax
from jax.experimental import pallas as pl
from jax.experimental.pallas import tpu as pltpu

F32 = jnp.float32
BF16 = jnp.bfloat16
HIGHEST = lax.Precision.HIGHEST

GRID_W = 64
N_LRU_BLOCKS = 8
CONV_W = 4
LRU_C = 8.0
HEAD_DIM = 64
N_KV_HEADS = 4
ROPE_THETA = 10000.0
N_EXPERT_GROUPS = 4
N_MOD = 6
EPS = 1e-6

LANES = 128
SUBLANES = 8
BF16_SUBLANES = 16
VMEM_LIMIT = 48 * 1024 * 1024
VMEM_LIMIT_MOE = 56 * 1024 * 1024

TM_PROJ = 512
TC_SCAN = 512
TQ_ATTN = 1024
ATTN_QSUB = 256
ATTN_SLAB = 256
TM_MOE = 1024
MOE_BLK = 128
MOE_CHUNK = 256


def _cparams(sem, vmem_limit=VMEM_LIMIT):
    return pltpu.CompilerParams(dimension_semantics=sem, vmem_limit_bytes=vmem_limit)


def _sigmoid(x):
    return 1.0 / (1.0 + jnp.exp(-x))


def _gelu_tanh(x):
    c = math.sqrt(2.0 / math.pi)
    return x * (0.5 * (1.0 + jnp.tanh(c * (x + 0.044715 * (x * x * x)))))


def _modulate(xf, g, shift, scale):
    ms = jnp.mean(xf * xf, axis=-1, keepdims=True)
    y = xf * lax.rsqrt(ms + EPS) * g
    return y * (1.0 + scale) + shift


def _mod_spec(mod, d):
    if mod.shape[0] == 1:
        return pl.BlockSpec((None, N_MOD, d), lambda b, i: (0, 0, 0))
    return pl.BlockSpec((None, N_MOD, d), lambda b, i: (b, 0, 0))


def _ada_kernel(cond_ref, w_ref, b_ref, o_ref):
    cnd = cond_ref[...]
    s = cnd * _sigmoid(cnd)
    o_ref[...] = jnp.dot(s, w_ref[...], precision=HIGHEST, preferred_element_type=F32) + b_ref[...]


def _ada(cond, ada_w, ada_b):
    depth, d, n = ada_w.shape
    rows = cond.shape[0]
    tn = 1536
    return pl.pallas_call(
        _ada_kernel,
        grid=(depth, n // tn),
        in_specs=[
            pl.BlockSpec((rows, d), lambda l, j: (0, 0)),
            pl.BlockSpec((None, d, tn), lambda l, j: (l, 0, j)),
            pl.BlockSpec((None, 1, tn), lambda l, j: (l, 0, j)),
        ],
        out_specs=pl.BlockSpec((None, rows, tn), lambda l, j: (l, 0, j)),
        out_shape=jax.ShapeDtypeStruct((depth, rows, n), F32),
        compiler_params=_cparams(("arbitrary", "arbitrary")),
        name="ada",
    )(cond, ada_w, ada_b.reshape(depth, 1, n))


def _inproj_kernel(x_ref, mod_ref, nw_ref, w_ref, g_ref, u_ref):
    d = x_ref.shape[-1]
    h = _modulate(x_ref[...], nw_ref[...], mod_ref[0:1, :], mod_ref[1:2, :])
    r = jnp.dot(h.astype(BF16), w_ref[...], preferred_element_type=F32)
    g_ref[...] = _gelu_tanh(r[:, :d]).astype(BF16)
    u_ref[...] = r[:, d:]


def _inproj(x, mod, norm_w, w_bf):
    b, s, d = x.shape
    tm = min(TM_PROJ, s)
    row = pl.BlockSpec((None, tm, d), lambda bi, i: (bi, i, 0))
    return pl.pallas_call(
        _inproj_kernel,
        grid=(b, s // tm),
        in_specs=[
            row,
            _mod_spec(mod, d),
            pl.BlockSpec((1, d), lambda bi, i: (0, 0)),
            pl.BlockSpec((d, 2 * d), lambda bi, i: (0, 0)),
        ],
        out_specs=[row, row],
        out_shape=[jax.ShapeDtypeStruct((b, s, d), BF16), jax.ShapeDtypeStruct((b, s, d), F32)],
        compiler_params=_cparams(("parallel", "parallel")),
        name="lru_inproj",
    )(x, mod, norm_w.reshape(1, d), w_bf)


def _scan_kernel(u_ref, up_ref, un_ref, h0_ref, cw_ref, cb_ref, wg_ref, bg_ref, lam_ref,
                 o_ref, hfin_ref, ubuf, a_s, b_s, carry):
    tc, d = u_ref.shape
    direction = pl.program_id(1)
    j = pl.program_id(2)
    n = pl.num_programs(2)
    chunk = jnp.where(direction == 0, j, n - 1 - j)

    @pl.when(j == 0)
    def _():
        carry[...] = h0_ref[...]

    ubuf[0:SUBLANES, :] = jnp.where(chunk > 0, up_ref[...], 0.0)
    ubuf[SUBLANES:SUBLANES + tc, :] = u_ref[...]
    ubuf[SUBLANES + tc:2 * SUBLANES + tc, :] = jnp.where(chunk < n - 1, un_ref[...], 0.0)
    left = CONV_W // 2
    uc = cb_ref[...]
    for k in range(CONV_W):
        uc = uc + ubuf[pl.ds(SUBLANES - left + k, tc), :] * cw_ref[k:k + 1, :]

    neg_lam = -lam_ref[...]
    softplus = jnp.maximum(neg_lam, 0.0) + jnp.log1p(jnp.exp(-jnp.abs(neg_lam)))
    decay = -LRU_C * softplus
    ub = uc.astype(BF16)
    blk = d // N_LRU_BLOCKS
    for nb in range(N_LRU_BLOCKS):
        sl = slice(blk * nb, blk * (nb + 1))
        z = jnp.dot(ub[:, sl], wg_ref[nb], preferred_element_type=F32)
        r = _sigmoid(z[:, :blk] + bg_ref[0:1, sl])
        ig = _sigmoid(z[:, blk:] + bg_ref[1:2, sl])
        log_a = decay[:, sl] * r
        a = jnp.exp(log_a)
        a_s[:, sl] = a
        b_s[:, sl] = jnp.sqrt(-jnp.tanh(log_a) * (a * a + 1.0)) * (ig * uc[:, sl])

    row = lax.broadcasted_iota(jnp.int32, (SUBLANES, d), 0)
    n_tiles = tc // SUBLANES

    def tile_scan(r0, h, reverse):
        a = a_s[pl.ds(r0, SUBLANES), :]
        bv = b_s[pl.ds(r0, SUBLANES), :]
        for s in (1, 2, 4):
            if reverse:
                valid = row < SUBLANES - s
                shift = SUBLANES - s
            else:
                valid = row >= s
                shift = s
            a_sh = jnp.where(valid, pltpu.roll(a, shift, 0), 1.0)
            b_sh = jnp.where(valid, pltpu.roll(bv, shift, 0), 0.0)
            bv = a * b_sh + bv
            a = a * a_sh
        hh = a * h + bv
        o_ref[pl.ds(r0, SUBLANES), :] = hh
        return hh[0:1, :] if reverse else hh[SUBLANES - 1:SUBLANES, :]

    @pl.when(direction == 0)
    def _():
        def body(i, h):
            return tile_scan(pl.multiple_of(i * SUBLANES, SUBLANES), h, False)
        carry[...] = lax.fori_loop(0, n_tiles, body, carry[...], unroll=4)

    @pl.when(direction == 1)
    def _():
        def body(i, h):
            return tile_scan(pl.multiple_of((n_tiles - 1 - i) * SUBLANES, SUBLANES), h, True)
        carry[...] = lax.fori_loop(0, n_tiles, body, carry[...], unroll=4)

    hfin_ref[...] = carry[...]


def _lru_scan(u, h0, conv_w, conv_b, wg_bf, bg, lam):
    b, s, d = u.shape
    tc = min(TC_SCAN, s)
    n = s // tc
    per = tc // SUBLANES
    last8 = s // SUBLANES - 1
    blk = d // N_LRU_BLOCKS

    def chunk_of(di, j):
        return j + di * (n - 1 - 2 * j)

    return pl.pallas_call(
        _scan_kernel,
        grid=(b, 2, n),
        in_specs=[
            pl.BlockSpec((None, tc, d), lambda bi, di, j: (bi, chunk_of(di, j), 0)),
            pl.BlockSpec((None, SUBLANES, d),
                         lambda bi, di, j: (bi, jnp.maximum(chunk_of(di, j) * per - 1, 0), 0)),
            pl.BlockSpec((None, SUBLANES, d),
                         lambda bi, di, j: (bi, jnp.minimum((chunk_of(di, j) + 1) * per, last8), 0)),
            pl.BlockSpec((None, None, 1, d), lambda bi, di, j: (di, bi, 0, 0)),
            pl.BlockSpec((CONV_W, d), lambda bi, di, j: (0, 0)),
            pl.BlockSpec((1, d), lambda bi, di, j: (0, 0)),
            pl.BlockSpec((None, N_LRU_BLOCKS, blk, 2 * blk), lambda bi, di, j: (di, 0, 0, 0)),
            pl.BlockSpec((None, 2, d), lambda bi, di, j: (di, 0, 0)),
            pl.BlockSpec((None, 1, d), lambda bi, di, j: (di, 0, 0)),
        ],
        out_specs=[
            pl.BlockSpec((None, None, tc, d), lambda bi, di, j: (di, bi, chunk_of(di, j), 0)),
            pl.BlockSpec((None, None, 1, d), lambda bi, di, j: (di, bi, 0, 0)),
        ],
        out_shape=[jax.ShapeDtypeStruct((2, b, s, d), F32), jax.ShapeDtypeStruct((2, b, 1, d), F32)],
        scratch_shapes=[
            pltpu.VMEM((tc + 2 * SUBLANES, d), F32),
            pltpu.VMEM((tc, d), F32),
            pltpu.VMEM((tc, d), F32),
            pltpu.VMEM((1, d), F32),
        ],
        compiler_params=_cparams(("arbitrary", "arbitrary", "arbitrary")),
        name="lru_scan",
    )(u, u, u, h0, conv_w, conv_b.reshape(1, d), wg_bf, bg, lam.reshape(2, 1, d))


def _lru_out_kernel(g_ref, hf_ref, hb_ref, x_ref, mod_ref, w_ref, o_ref):
    y = hf_ref[...] + hb_ref[...]
    t = (g_ref[...].astype(F32) * y).astype(BF16)
    o = jnp.dot(t, w_ref[...], preferred_element_type=F32)
    o_ref[...] = x_ref[...] + mod_ref[2:3, :] * o


def _lru_out(g, h, x, mod, w_bf):
    b, s, d = x.shape
    tm = min(TM_PROJ, s)
    row = pl.BlockSpec((None, tm, d), lambda bi, i: (bi, i, 0))
    return pl.pallas_call(
        _lru_out_kernel,
        grid=(b, s // tm),
        in_specs=[
            row,
            pl.BlockSpec((None, None, tm, d), lambda bi, i: (0, bi, i, 0)),
            pl.BlockSpec((None, None, tm, d), lambda bi, i: (1, bi, i, 0)),
            row,
            _mod_spec(mod, d),
            pl.BlockSpec((d, d), lambda bi, i: (0, 0)),
        ],
        out_specs=row,
        out_shape=jax.ShapeDtypeStruct((b, s, d), F32),
        compiler_params=_cparams(("parallel", "parallel")),
        name="lru_out",
    )(g, h, h, x, mod, w_bf)


def _attn_out_kernel(a_ref, x_ref, mod_ref, w_ref, o_ref):
    o = jnp.dot(a_ref[...], w_ref[...], preferred_element_type=F32)
    o_ref[...] = x_ref[...] + mod_ref[2:3, :] * o


def _attn_out(a, x, mod, w_bf):
    b, s, d = x.shape
    tm = min(TM_PROJ, s)
    row = pl.BlockSpec((None, tm, d), lambda bi, i: (bi, i, 0))
    return pl.pallas_call(
        _attn_out_kernel,
        grid=(b, s // tm),
        in_specs=[row, row, _mod_spec(mod, d), pl.BlockSpec((d, d), lambda bi, i: (0, 0))],
        out_specs=row,
        out_shape=jax.ShapeDtypeStruct((b, s, d), F32),
        compiler_params=_cparams(("parallel", "parallel")),
        name="attn_out",
    )(a, x, mod, w_bf)


def _first_hit(vals, target):
    hits = []
    found = None
    for v in vals:
        eq = v == target
        if found is None:
            hits.append(eq)
            found = eq
        else:
            hits.append(jnp.logical_and(eq, jnp.logical_not(found)))
            found = jnp.logical_or(found, eq)
    return hits


def _route_kernel(x_ref, mod_ref, nw_ref, rwt_ref, rb_ref, tri_ref, h_ref, meta_ref, stat_ref):
    n_exp = rwt_ref.shape[0] // 2
    tm = x_ref.shape[0]
    per_group = n_exp // N_EXPERT_GROUPS
    h = _modulate(x_ref[...], nw_ref[...], mod_ref[3:4, :], mod_ref[4:5, :])
    h_hi = h.astype(BF16)
    h_ref[...] = h_hi
    h_lo = (h - h_hi.astype(F32)).astype(BF16)
    nt = (((1,), (1,)), ((), ()))
    by_hi = lax.dot_general(rwt_ref[...], h_hi, nt, preferred_element_type=F32)
    by_lo = lax.dot_general(rwt_ref[0:n_exp, :], h_lo, nt, preferred_element_type=F32)
    logits = by_hi[:n_exp, :] + by_hi[n_exp:, :] + by_lo + rb_ref[...]
    rows = [logits[e:e + 1, :] for e in range(n_exp)]
    mx = functools.reduce(jnp.maximum, rows)
    ex = [jnp.exp(r - mx) for r in rows]
    z = functools.reduce(lambda p, q: p + q, ex)
    probs = [e_ / z for e_ in ex]
    scores = []
    for g in range(N_EXPERT_GROUPS):
        v0, v1, v2, v3 = probs[per_group * g:per_group * (g + 1)]
        hi01, lo01 = jnp.maximum(v0, v1), jnp.minimum(v0, v1)
        hi23, lo23 = jnp.maximum(v2, v3), jnp.minimum(v2, v3)
        top1 = jnp.maximum(hi01, hi23)
        top2 = jnp.maximum(jnp.minimum(hi01, hi23), jnp.maximum(lo01, lo23))
        scores.append(top1 + top2)
    best = functools.reduce(jnp.maximum, scores)
    sel = _first_hit(scores, best)
    masked = [jnp.where(sel[e // per_group], probs[e], -1.0) for e in range(n_exp)]
    m1 = functools.reduce(jnp.maximum, masked)
    is1 = _first_hit(masked, m1)
    masked2 = [jnp.where(is1[e], -2.0, masked[e]) for e in range(n_exp)]
    m2 = functools.reduce(jnp.maximum, masked2)
    is2 = _first_hit(masked2, m2)
    denom = m1 + m2
    gates = [jnp.where(is1[e], m1, jnp.where(is2[e], m2, 0.0)) / denom for e in range(n_exp)]
    zero = jnp.zeros_like(denom)
    for k in range(per_group):
        meta_ref[k:k + 1, :] = functools.reduce(
            lambda p, q: p + q,
            [jnp.where(sel[g], gates[per_group * g + k], 0.0) for g in range(N_EXPERT_GROUPS)])

    onehot = [jnp.where(sel[g], 1.0, 0.0) for g in range(N_EXPERT_GROUPS)]
    onehot_t = jnp.concatenate(onehot + [zero] * (SUBLANES - N_EXPERT_GROUPS), axis=0).astype(BF16)
    earlier = jnp.dot(onehot_t, tri_ref[...], preferred_element_type=F32)
    start = jnp.zeros((1, 1), F32)
    pos = zero
    starts, blocks = [], []
    for g in range(N_EXPERT_GROUPS):
        count = jnp.sum(onehot[g], axis=1, keepdims=True)
        n_blk = jnp.floor((count + (MOE_BLK - 1)) * (1.0 / MOE_BLK))
        pos = pos + onehot[g] * (start + earlier[g:g + 1, :])
        starts.append(start)
        blocks.append(n_blk)
        start = start + n_blk * MOE_BLK
    meta_ref[per_group:per_group + 1, :] = pos
    meta_ref[per_group + 1:SUBLANES, :] = jnp.zeros((SUBLANES - per_group - 1, tm), F32)
    stat_ref[...] = jnp.concatenate([jnp.broadcast_to(v, (1, LANES)) for v in starts + blocks], axis=0)


def _route(x, mod, norm_w, router_w, router_b):
    b, s, d = x.shape
    n_exp = router_w.shape[1]
    assert n_exp // N_EXPERT_GROUPS == 4 and N_EXPERT_GROUPS == 4
    tm = min(TM_MOE, s)
    n_s = s // tm
    row = pl.BlockSpec((None, tm, d), lambda bi, i: (bi, i, 0))
    t_idx = jnp.arange(tm, dtype=jnp.int32)
    tri = (t_idx[:, None] < t_idx[None, :]).astype(BF16)
    rw_t = router_w.T
    rw_hi = rw_t.astype(BF16)
    rw_halves = jnp.concatenate([rw_hi, (rw_t - rw_hi.astype(F32)).astype(BF16)], axis=0)
    return pl.pallas_call(
        _route_kernel,
        grid=(b, n_s),
        in_specs=[
            row,
            _mod_spec(mod, d),
            pl.BlockSpec((1, d), lambda bi, i: (0, 0)),
            pl.BlockSpec((2 * n_exp, d), lambda bi, i: (0, 0)),
            pl.BlockSpec((n_exp, 1), lambda bi, i: (0, 0)),
            pl.BlockSpec((tm, tm), lambda bi, i: (0, 0)),
        ],
        out_specs=[row,
                   pl.BlockSpec((SUBLANES, tm), lambda bi, i: (0, bi * n_s + i)),
                   pl.BlockSpec((None, SUBLANES, LANES), lambda bi, i: (bi * n_s + i, 0, 0))],
        out_shape=[jax.ShapeDtypeStruct((b, s, d), BF16),
                   jax.ShapeDtypeStruct((SUBLANES, b * s), F32),
                   jax.ShapeDtypeStruct((b * n_s, SUBLANES, LANES), F32)],
        compiler_params=_cparams(("parallel", "parallel")),
        name="moe_route",
    )(x, mod, norm_w.reshape(1, d), rw_halves, router_b.reshape(n_exp, 1), tri)


def _moe_kernel(stat_ref, h_ref, mrow_ref, mcol_ref, gcol_ref, x_ref, mod_ref, w1_ref, w3_ref, w2_ref,
                o_ref, xc, gc, yc):
    pair = pl.program_id(0)
    step = pl.program_id(1)
    tm = h_ref.shape[0]
    rows = xc.shape[1]
    per_group = N_EXPERT_GROUPS
    n_gather, n_expert = 2, 2 * N_EXPERT_GROUPS

    for slot in range(2):
        @pl.when(step == slot)
        def _(slot=slot):
            pos_row = mrow_ref[per_group:per_group + 1, :].astype(jnp.int32)
            h = h_ref[...]
            gcol = gcol_ref[...]
            for c in range(rows // MOE_CHUNK):
                r_id = lax.broadcasted_iota(jnp.int32, (MOE_CHUNK, tm), 0) + c * MOE_CHUNK
                perm = jnp.where(pos_row == r_id, 1.0, 0.0).astype(BF16)
                sl = slice(c * MOE_CHUNK, (c + 1) * MOE_CHUNK)
                xc[slot, sl, :] = jnp.dot(perm, h, preferred_element_type=F32).astype(BF16)
                gc[slot, sl, :] = jnp.dot(perm, gcol, preferred_element_type=F32)
            yc[slot] = jnp.zeros((rows, yc.shape[2]), BF16)

    @pl.when(jnp.logical_and(step >= n_gather, step < n_gather + n_expert))
    def _():
        e_step = step - n_gather
        group = e_step // 2
        slot = ((e_step + 1) // 2) % 2
        base = (2 * pair + slot) * SUBLANES
        start = stat_ref[base + group]
        n_blk = stat_ref[base + per_group + group]

        def experts(r0, n_rows):
            xb = xc[slot, pl.ds(r0, n_rows), :]
            g = gc[slot, pl.ds(r0, n_rows), :]
            parts = []
            for k in range(per_group):
                a = jnp.dot(xb, w1_ref[k], preferred_element_type=F32)
                c = jnp.dot(xb, w3_ref[k], preferred_element_type=F32)
                gate = g[:, k:k + 1] + g[:, per_group + k:per_group + k + 1]
                parts.append(((a * _sigmoid(a)) * c * gate).astype(BF16))
            t = jnp.concatenate(parts, axis=1)
            yc[slot, pl.ds(r0, n_rows), :] = jnp.dot(t, w2_ref[...], preferred_element_type=F32).astype(BF16)

        def block_pair(j, carry):
            experts(pl.multiple_of(start + j * (2 * MOE_BLK), MOE_BLK), 2 * MOE_BLK)
            return carry

        lax.fori_loop(0, n_blk // 2, block_pair, 0)

        @pl.when(n_blk % 2 == 1)
        def _():
            experts(pl.multiple_of(start + (n_blk - 1) * MOE_BLK, MOE_BLK), MOE_BLK)

    for slot in range(2):
        @pl.when(step == n_gather + n_expert + slot)
        def _(slot=slot):
            y_sorted = yc[slot]
            for c in range(tm // MOE_CHUNK):
                sl = slice(c * MOE_CHUNK, (c + 1) * MOE_CHUNK)
                pos_col = mcol_ref[sl, per_group:per_group + 1].astype(jnp.int32)
                c_id = lax.broadcasted_iota(jnp.int32, (MOE_CHUNK, rows), 1)
                perm_t = jnp.where(pos_col == c_id, 1.0, 0.0).astype(BF16)
                y = jnp.dot(perm_t, y_sorted, preferred_element_type=F32)
                o_ref[sl, :] = x_ref[sl, :] + mod_ref[5:6, :] * y


def _moe(h, meta, stat, x, mod, w1_bf, w3_bf, w2g):
    b, s, d = x.shape
    n_tok = b * s
    tm = min(TM_MOE, s)
    n_tiles = n_tok // tm
    assert n_tiles % 2 == 0
    rows = tm + N_EXPERT_GROUPS * MOE_BLK
    rows = -(-rows // MOE_CHUNK) * MOE_CHUNK
    per_group, ff = w1_bf.shape[0] // N_EXPERT_GROUPS, w1_bf.shape[2]
    n_gather, n_expert = 2, 2 * N_EXPERT_GROUPS
    n_steps = n_gather + n_expert + 2

    def gather_tile(p, t):
        return 2 * p + jnp.minimum(t, 1)

    def scatter_tile(p, t):
        return 2 * p + jnp.where(t == n_steps - 1, 1, 0)

    def group_of(t):
        return jnp.clip((t - n_gather) // 2, 0, N_EXPERT_GROUPS - 1)

    if mod.shape[0] == 1:
        mod_spec = pl.BlockSpec((None, N_MOD, d), lambda p, t, st: (0, 0, 0))
    else:
        mod_spec = pl.BlockSpec((None, N_MOD, d), lambda p, t, st: ((scatter_tile(p, t) * tm) // s, 0, 0))
    meta_col = meta.T
    gates = meta_col[:, :N_EXPERT_GROUPS]
    g_hi = gates.astype(BF16)
    g_lo = (gates - g_hi.astype(F32)).astype(BF16)
    gcol = jnp.concatenate([g_hi, g_lo, jnp.zeros((n_tok, LANES - 2 * N_EXPERT_GROUPS), BF16)], axis=1)
    stat_i = stat[:, :, 0].astype(jnp.int32).reshape(-1)
    once = pl.Buffered(1)

    grid_spec = pltpu.PrefetchScalarGridSpec(
        num_scalar_prefetch=1,
        grid=(n_tiles // 2, n_steps),
        in_specs=[
            pl.BlockSpec((tm, d), lambda p, t, st: (gather_tile(p, t), 0), pipeline_mode=once),
            pl.BlockSpec((SUBLANES, tm), lambda p, t, st: (0, gather_tile(p, t))),
            pl.BlockSpec((tm, SUBLANES), lambda p, t, st: (scatter_tile(p, t), 0)),
            pl.BlockSpec((tm, LANES), lambda p, t, st: (gather_tile(p, t), 0)),
            pl.BlockSpec((tm, d), lambda p, t, st: (scatter_tile(p, t), 0), pipeline_mode=once),
            mod_spec,
            pl.BlockSpec((per_group, d, ff), lambda p, t, st: (group_of(t), 0, 0)),
            pl.BlockSpec((per_group, d, ff), lambda p, t, st: (group_of(t), 0, 0)),
            pl.BlockSpec((None, per_group * ff, d), lambda p, t, st: (group_of(t), 0, 0)),
        ],
        out_specs=pl.BlockSpec((tm, d), lambda p, t, st: (scatter_tile(p, t), 0), pipeline_mode=once),
        scratch_shapes=[pltpu.VMEM((2, rows, d), BF16), pltpu.VMEM((2, rows, LANES), F32),
                        pltpu.VMEM((2, rows, d), BF16)],
    )
    out = pl.pallas_call(
        _moe_kernel,
        grid_spec=grid_spec,
        out_shape=jax.ShapeDtypeStruct((n_tok, d), F32),
        compiler_params=_cparams(("parallel", "arbitrary"), VMEM_LIMIT_MOE),
        name="moe_experts",
    )(stat_i, h.reshape(n_tok, d), meta, meta_col, gcol, x.reshape(n_tok, d), mod, w1_bf, w3_bf, w2g)
    return out.reshape(b, s, d)


def _group_expert_weights(w1, w3, w2):
    d = w2.shape[2]
    return w1.astype(BF16), w3.astype(BF16), w2.astype(BF16).reshape(N_EXPERT_GROUPS, -1, d)


def _moe_block(x, mod, norm_w, router_w, router_b, w1_bf, w3_bf, w2g):
    shape = x.shape
    if mod.shape[0] == 1:
        x = x.reshape(1, -1, shape[-1])
    h, meta, stat = _route(x, mod, norm_w, router_w, router_b)
    return _moe(h, meta, stat, x, mod, w1_bf, w3_bf, w2g).reshape(shape)


def _head_norm(t, w_row, ones_blockdiag):
    ss = jnp.dot((t * t).astype(BF16), ones_blockdiag, preferred_element_type=F32)
    return t * lax.rsqrt(ss * (1.0 / HEAD_DIM) + EPS) * w_row


def _rope(t, cos, sin_next, sin_prev):
    quarter = HEAD_DIM // 4
    outs = []
    for nb in range(t.shape[1] // LANES):
        tb = t[:, LANES * nb:LANES * (nb + 1)]
        outs.append(tb * cos
                    + pltpu.roll(tb, LANES - quarter, 1) * sin_next
                    + pltpu.roll(tb, quarter, 1) * sin_prev)
    return jnp.concatenate(outs, axis=1)


def _qkv_kernel(x_ref, mod_ref, nw_ref, w_ref, qn_ref, kn_ref, ones_ref, cos_ref, sn_ref, sp_ref,
                q_ref, k_ref, v_ref):
    n_q = q_ref.shape[-1]
    n_kv = k_ref.shape[-1]
    h = _modulate(x_ref[...], nw_ref[...], mod_ref[0:1, :], mod_ref[1:2, :])
    r = jnp.dot(h.astype(BF16), w_ref[...], preferred_element_type=F32)
    cos, sn, sp = cos_ref[...], sn_ref[...], sp_ref[...]
    q = _head_norm(r[:, :n_q], qn_ref[...], ones_ref[...])
    q_ref[...] = (_rope(q, cos, sn, sp) * (math.log2(math.e) / math.sqrt(HEAD_DIM))).astype(BF16)
    k = _head_norm(r[:, n_q:n_q + n_kv], kn_ref[...], ones_ref[0:n_kv, 0:n_kv])
    k_ref[...] = _rope(k, cos, sn, sp).astype(BF16)
    v_ref[...] = r[:, n_q + n_kv:].astype(BF16)


def _kv_ctx_kernel(x_ref, mod_ref, nw_ref, w_ref, kn_ref, ones_ref, k_ref, v_ref):
    n_kv = k_ref.shape[-1]
    h = _modulate(x_ref[...], nw_ref[...], mod_ref[0:1, :], mod_ref[1:2, :])
    r = jnp.dot(h.astype(BF16), w_ref[...], preferred_element_type=F32)
    k_ref[...] = _head_norm(r[:, :n_kv], kn_ref[...], ones_ref[...]).astype(BF16)
    v_ref[...] = r[:, n_kv:].astype(BF16)


def _rope_tables(s):
    half = HEAD_DIM // 2
    quarter = half // 2
    t = jnp.arange(s, dtype=jnp.int32)
    pos_r = (t // GRID_W).astype(F32)[:, None]
    pos_c = (t % GRID_W).astype(F32)[:, None]
    lane = jnp.arange(LANES, dtype=jnp.int32)[None, :]
    in_head = lane % HEAD_DIM
    in_half = in_head % half
    freq = (in_half % quarter).astype(F32)
    inv = ROPE_THETA ** (-freq / quarter)
    ang = jnp.where(in_head < half, pos_r, pos_c) * inv
    cos = jnp.cos(ang)
    sin = jnp.sin(ang)
    first = in_half < quarter
    return cos, jnp.where(first, -sin, 0.0), jnp.where(first, 0.0, sin)


def _ones_blockdiag(n):
    i = jnp.arange(n, dtype=jnp.int32)
    return (i[:, None] // HEAD_DIM == i[None, :] // HEAD_DIM).astype(BF16)


def _qkv(x, mod, norm_w, w_bf, qn_w, kn_w):
    b, s, d = x.shape
    n_kv = N_KV_HEADS * HEAD_DIM
    n_q = w_bf.shape[1] - 2 * n_kv
    tm = min(TM_PROJ, s)
    cos, sn, sp = _rope_tables(s)
    tab = pl.BlockSpec((tm, LANES), lambda i, bi: (i, 0))

    def row(w):
        return pl.BlockSpec((None, tm, w), lambda i, bi: (bi, i, 0))

    def const(shape):
        return pl.BlockSpec(shape, lambda i, bi: (0,) * len(shape))

    return pl.pallas_call(
        _qkv_kernel,
        grid=(s // tm, b),
        in_specs=[
            row(d),
            pl.BlockSpec((None, N_MOD, d), lambda i, bi: (bi, 0, 0)),
            const((1, d)),
            const((d, n_q + 2 * n_kv)),
            const((1, n_q)),
            const((1, n_kv)),
            const((n_q, n_q)),
            tab, tab, tab,
        ],
        out_specs=[row(n_q), row(n_kv), row(n_kv)],
        out_shape=[jax.ShapeDtypeStruct((b, s, n_q), BF16),
                   jax.ShapeDtypeStruct((b, s, n_kv), BF16),
                   jax.ShapeDtypeStruct((b, s, n_kv), BF16)],
        compiler_params=_cparams(("parallel", "parallel")),
        name="attn_qkv",
    )(x, mod, norm_w.reshape(1, d), w_bf,
      jnp.tile(qn_w, n_q // HEAD_DIM).reshape(1, n_q), jnp.tile(kn_w, N_KV_HEADS).reshape(1, n_kv),
      _ones_blockdiag(n_q), cos, sn, sp)


def _kv_ctx(ctx, mod, norm_w, w_kv_bf, kn_w):
    b, s, d = ctx.shape
    n_kv = N_KV_HEADS * HEAD_DIM
    tm = min(TM_PROJ, s)

    def row(w):
        return pl.BlockSpec((None, tm, w), lambda bi, i: (bi, i, 0))

    def const(shape):
        return pl.BlockSpec(shape, lambda bi, i: (0,) * len(shape))

    return pl.pallas_call(
        _kv_ctx_kernel,
        grid=(b, s // tm),
        in_specs=[row(d), _mod_spec(mod, d), const((1, d)), const((d, 2 * n_kv)),
                  const((1, n_kv)), const((n_kv, n_kv))],
        out_specs=[row(n_kv), row(n_kv)],
        out_shape=[jax.ShapeDtypeStruct((b, s, n_kv), BF16), jax.ShapeDtypeStruct((b, s, n_kv), BF16)],
        compiler_params=_cparams(("parallel", "parallel")),
        name="attn_kv_ctx",
    )(ctx, mod, norm_w.reshape(1, d), w_kv_bf, jnp.tile(kn_w, N_KV_HEADS).reshape(1, n_kv),
      _ones_blockdiag(n_kv))


def _attn_kernel(q_ref, k_ref, vt_ref, o_ref, s_scr):
    q = q_ref[...]
    k = k_ref[...]
    vt = vt_ref[...]
    n_keys = k.shape[0]
    n_head_pairs = q.shape[1] // (2 * HEAD_DIM)
    n_units = n_head_pairs * (q.shape[0] // ATTN_QSUB)

    def scores(u):
        rows = slice(ATTN_QSUB * (u // n_head_pairs), ATTN_QSUB * (u // n_head_pairs + 1))
        h0 = 2 * (u % n_head_pairs)
        q2 = jnp.concatenate([q[rows, HEAD_DIM * (h0 + i):HEAD_DIM * (h0 + i + 1)] for i in range(2)], axis=0)
        st = lax.dot_general(k, q2, (((1,), (1,)), ((), ())), preferred_element_type=F32)
        s_scr[u % 2] = st
        slabs = st.reshape(n_keys // ATTN_SLAB, ATTN_SLAB, 2 * ATTN_QSUB)
        return jnp.max(jnp.max(slabs, axis=0), axis=0, keepdims=True)

    def weighted_values(u, m):
        p = jnp.exp2(s_scr[u % 2] - m).astype(BF16)
        ot = jnp.dot(vt, p, preferred_element_type=F32)
        on = ot[:HEAD_DIM, :] / ot[HEAD_DIM:HEAD_DIM + 1, :]
        return [on[:, :ATTN_QSUB], on[:, ATTN_QSUB:]]

    outs = []
    m_next = scores(0)
    for u in range(n_units):
        m = m_next
        if u + 1 < n_units:
            m_next = scores(u + 1)
        outs += weighted_values(u, m)
        if (u + 1) % n_head_pairs == 0:
            rows = slice(ATTN_QSUB * (u // n_head_pairs), ATTN_QSUB * (u // n_head_pairs + 1))
            o_ref[rows, :] = jnp.concatenate(outs, axis=0).T.astype(BF16)
            outs = []


def _attention(q, k, vt):
    assert k.shape[2] % ATTN_SLAB == 0
    b, s, n_q = q.shape
    n_keys = k.shape[2]
    group_w = n_q // N_KV_HEADS
    tq = min(TQ_ATTN, s)
    return pl.pallas_call(
        _attn_kernel,
        grid=(b, N_KV_HEADS, s // tq),
        in_specs=[
            pl.BlockSpec((None, tq, group_w), lambda bi, g, i: (bi, i, g)),
            pl.BlockSpec((None, None, n_keys, HEAD_DIM), lambda bi, g, i: (bi, g, 0, 0)),
            pl.BlockSpec((None, None, vt.shape[2], n_keys), lambda bi, g, i: (bi, g, 0, 0)),
        ],
        out_specs=pl.BlockSpec((None, tq, group_w), lambda bi, g, i: (bi, i, g)),
        out_shape=jax.ShapeDtypeStruct((b, s, n_q), BF16),
        scratch_shapes=[pltpu.VMEM((2, n_keys, 2 * ATTN_QSUB), F32)],
        compiler_params=_cparams(("parallel", "parallel", "parallel")),
        name="attn_core",
    )(q, k, vt)


def _modulations(c, c_ctx, ada_w, ada_b):
    b, d = c.shape
    depth = ada_w.shape[0]
    assert b <= SUBLANES
    cond = jnp.zeros((2 * SUBLANES, d), F32).at[:b].set(c).at[b].set(c_ctx)
    mods = _ada(cond, ada_w, ada_b)
    mod_x = [mods[l, :b].reshape(b, N_MOD, d) for l in range(depth)]
    mod_c = [mods[l, b:b + 1].reshape(1, N_MOD, d) for l in range(depth)]
    return mod_x, mod_c


def _lru_layer(x, ctx, mod_x, mod_c, norm_w, in_w, conv_w, conv_b, ga_w, ga_b, gx_w, gx_b, lam, out_w):
    b, _, d = x.shape
    in_bf = in_w.astype(BF16)
    out_bf = out_w.astype(BF16)
    wg = jnp.concatenate([ga_w, gx_w], axis=-1).astype(BF16)
    bg = jnp.stack([ga_b, gx_b], axis=1)
    g_c, u_c = _inproj(ctx, mod_c, norm_w, in_bf)
    g_x, u_x = _inproj(x, mod_x, norm_w, in_bf)
    h_c, h_c_end = _lru_scan(u_c, jnp.zeros((2, b, 1, d), F32), conv_w, conv_b, wg, bg, lam)
    h_x, _ = _lru_scan(u_x, h_c_end, conv_w, conv_b, wg, bg, lam)
    return _lru_out(g_x, h_x, x, mod_x, out_bf), _lru_out(g_c, h_c, ctx, mod_c, out_bf)


def _attn_layer(x, ctx, mod_x, mod_c, norm_w, qkv_w, qn_w, kn_w, o_w):
    b, s, _ = x.shape
    n_kv = N_KV_HEADS * HEAD_DIM
    qkv_bf = qkv_w.astype(BF16)
    n_q = qkv_bf.shape[1] - 2 * n_kv
    q, k_x, v_x = _qkv(x, mod_x, norm_w, qkv_bf, qn_w, kn_w)
    k_c, v_c = _kv_ctx(ctx, mod_c, norm_w, qkv_bf[:, n_q:], kn_w)
    n_keys = ctx.shape[1] + s
    k_all = jnp.concatenate([k_c, k_x], axis=1).reshape(b, n_keys, N_KV_HEADS, HEAD_DIM)
    v_all = jnp.concatenate([v_c, v_x], axis=1).reshape(b, n_keys, N_KV_HEADS, HEAD_DIM)
    ones_row = jnp.zeros((b, N_KV_HEADS, BF16_SUBLANES, n_keys), BF16).at[:, :, 0].set(1.0)
    vt = jnp.concatenate([v_all.transpose(0, 2, 3, 1), ones_row], axis=2)
    att = _attention(q, k_all.transpose(0, 2, 1, 3), vt)
    return _attn_out(att, x, mod_x, o_w.astype(BF16))


def kernel(x, c, ctx, c_ctx, ada_w, ada_b, norm_mix_w, norm_ffn_w, lru_in_w, lru_conv_w, lru_conv_b, lru_gate_a_w, lru_gate_a_b, lru_gate_x_w, lru_gate_x_b, lru_lambda, lru_out_w, attn_qkv_w, attn_q_norm_w, attn_k_norm_w, attn_o_w, router_w, router_b, moe_w1, moe_w3, moe_w2):
    assert ada_w.shape[0] == 2
    mod_x, mod_c = _modulations(c, c_ctx, ada_w, ada_b)

    x, ctx = _lru_layer(x, ctx, mod_x[0], mod_c[0], norm_mix_w[0], lru_in_w[0], lru_conv_w[0], lru_conv_b[0],
                        lru_gate_a_w[0], lru_gate_a_b[0], lru_gate_x_w[0], lru_gate_x_b[0], lru_lambda[0],
                        lru_out_w[0])
    w1, w3, w2 = _group_expert_weights(moe_w1[0], moe_w3[0], moe_w2[0])
    ctx = _moe_block(ctx, mod_c[0], norm_ffn_w[0], router_w, router_b, w1, w3, w2)
    x = _moe_block(x, mod_x[0], norm_ffn_w[0], router_w, router_b, w1, w3, w2)

    x = _attn_layer(x, ctx, mod_x[1], mod_c[1], norm_mix_w[1], attn_qkv_w[0], attn_q_norm_w[0],
                    attn_k_norm_w[0], attn_o_w[0])
    w1, w3, w2 = _group_expert_weights(moe_w1[1], moe_w3[1], moe_w2[1])
    return _moe_block(x, mod_x[1], norm_ffn_w[1], router_w, router_b, w1, w3, w2)
```

```python
import functools
import math

import jax
import jax.numpy as jnp
from jax import lax
from jax.experimental import pallas as pl
from jax.experimental.pallas import tpu as pltpu

F32 = jnp.float32
BF16 = jnp.bfloat16
HIGHEST = lax.Precision.HIGHEST

GRID_W = 64
N_LRU_BLOCKS = 8
CONV_W = 4
LRU_C = 8.0
HEAD_DIM = 64
N_KV_HEADS = 4
ROPE_THETA = 10000.0
N_EXPERT_GROUPS = 4
N_MOD = 6
EPS = 1e-6

LANES = 128
SUBLANES = 8
BF16_SUBLANES = 16
VMEM_LIMIT = 48 * 1024 * 1024
VMEM_LIMIT_MOE = 56 * 1024 * 1024

TM_PROJ = 512
TC_SCAN = 512
TQ_ATTN = 1024
ATTN_QSUB = 256
ATTN_SLAB = 256
TM_MOE = 1024
MOE_BLK = 128
MOE_CHUNK = 256


def _cparams(sem, vmem_limit=VMEM_LIMIT):
    return pltpu.CompilerParams(dimension_semantics=sem, vmem_limit_bytes=vmem_limit)


def _sigmoid(x):
    return 1.0 / (1.0 + jnp.exp(-x))


def _gelu_tanh(x):
    c = math.sqrt(2.0 / math.pi)
    return x * (0.5 * (1.0 + jnp.tanh(c * (x + 0.044715 * (x * x * x)))))


def _modulate(xf, g, shift, scale):
    ms = jnp.mean(xf * xf, axis=-1, keepdims=True)
    y = xf * lax.rsqrt(ms + EPS) * g
    return y * (1.0 + scale) + shift


def _mod_spec(mod, d):
    if mod.shape[0] == 1:
        return pl.BlockSpec((None, N_MOD, d), lambda b, i: (0, 0, 0))
    return pl.BlockSpec((None, N_MOD, d), lambda b, i: (b, 0, 0))


def _ada_kernel(cond_ref, w_ref, b_ref, o_ref):
    cnd = cond_ref[...]
    s = cnd * _sigmoid(cnd)
    o_ref[...] = jnp.dot(s, w_ref[...], precision=HIGHEST, preferred_element_type=F32) + b_ref[...]


def _ada(cond, ada_w, ada_b):
    depth, d, n = ada_w.shape
    rows = cond.shape[0]
    tn = 1536
    return pl.pallas_call(
        _ada_kernel,
        grid=(depth, n // tn),
        in_specs=[
            pl.BlockSpec((rows, d), lambda l, j: (0, 0)),
            pl.BlockSpec((None, d, tn), lambda l, j: (l, 0, j)),
            pl.BlockSpec((None, 1, tn), lambda l, j: (l, 0, j)),
        ],
        out_specs=pl.BlockSpec((None, rows, tn), lambda l, j: (l, 0, j)),
        out_shape=jax.ShapeDtypeStruct((depth, rows, n), F32),
        compiler_params=_cparams(("arbitrary", "arbitrary")),
        name="ada",
    )(cond, ada_w, ada_b.reshape(depth, 1, n))


def _inproj_kernel(x_ref, mod_ref, nw_ref, w_ref, g_ref, u_ref):
    d = x_ref.shape[-1]
    h = _modulate(x_ref[...], nw_ref[...], mod_ref[0:1, :], mod_ref[1:2, :])
    r = jnp.dot(h.astype(BF16), w_ref[...], preferred_element_type=F32)
    g_ref[...] = _gelu_tanh(r[:, :d]).astype(BF16)
    u_ref[...] = r[:, d:].astype(BF16)


def _inproj(x, mod, norm_w, w_bf):
    b, s, d = x.shape
    tm = min(TM_PROJ, s)
    row = pl.BlockSpec((None, tm, d), lambda bi, i: (bi, i, 0))
    return pl.pallas_call(
        _inproj_kernel,
        grid=(b, s // tm),
        in_specs=[
            row,
            _mod_spec(mod, d),
            pl.BlockSpec((1, d), lambda bi, i: (0, 0)),
            pl.BlockSpec((d, 2 * d), lambda bi, i: (0, 0)),
        ],
        out_specs=[row, row],
        out_shape=[jax.ShapeDtypeStruct((b, s, d), BF16), jax.ShapeDtypeStruct((b, s, d), BF16)],
        compiler_params=_cparams(("parallel", "parallel")),
        name="lru_inproj",
    )(x, mod, norm_w.reshape(1, d), w_bf)


def _scan_kernel(u_ref, up_ref, un_ref, h0_ref, cw_ref, cb_ref, wg_ref, bg_ref, lam_ref,
                 o_ref, hfin_ref, ubuf, uc_cache, a_s, b_s, carry):
    tc, d = u_ref.shape
    halo = up_ref.shape[0]
    direction = pl.program_id(1)
    j = pl.program_id(2)
    n = pl.num_programs(2)
    chunk = jnp.where(direction == 0, j, n - 1 - j)

    @pl.when(j == 0)
    def _():
        carry[...] = h0_ref[...]

    @pl.when(direction == 0)
    def _():
        ubuf[0:halo, :] = jnp.where(chunk > 0, up_ref[...].astype(F32), 0.0)
        ubuf[halo:halo + tc, :] = u_ref[...].astype(F32)
        ubuf[halo + tc:2 * halo + tc, :] = jnp.where(chunk < n - 1, un_ref[...].astype(F32), 0.0)
        left = CONV_W // 2
        conv = cb_ref[...]
        for k in range(CONV_W):
            conv = conv + ubuf[pl.ds(halo - left + k, tc), :] * cw_ref[k:k + 1, :]
        uc_cache[chunk] = conv

    uc = uc_cache[chunk]

    neg_lam = -lam_ref[...]
    softplus = jnp.maximum(neg_lam, 0.0) + jnp.log1p(jnp.exp(-jnp.abs(neg_lam)))
    decay = -LRU_C * softplus
    ub = uc.astype(BF16)
    blk = d // N_LRU_BLOCKS
    for nb in range(N_LRU_BLOCKS):
        sl = slice(blk * nb, blk * (nb + 1))
        z = jnp.dot(ub[:, sl], wg_ref[nb], preferred_element_type=F32)
        r = _sigmoid(z[:, :blk] + bg_ref[0:1, sl])
        ig = _sigmoid(z[:, blk:] + bg_ref[1:2, sl])
        log_a = decay[:, sl] * r
        a = jnp.exp(log_a)
        a_s[:, sl] = a
        b_s[:, sl] = jnp.sqrt(-jnp.tanh(log_a) * (a * a + 1.0)) * (ig * uc[:, sl])

    row = lax.broadcasted_iota(jnp.int32, (SUBLANES, d), 0)
    n_tiles = tc // SUBLANES

    def tile_scan(r0, h, reverse):
        a = a_s[pl.ds(r0, SUBLANES), :]
        bv = b_s[pl.ds(r0, SUBLANES), :]
        for s in (1, 2, 4):
            if reverse:
                valid = row < SUBLANES - s
                shift = SUBLANES - s
            else:
                valid = row >= s
                shift = s
            a_sh = jnp.where(valid, pltpu.roll(a, shift, 0), 1.0)
            b_sh = jnp.where(valid, pltpu.roll(bv, shift, 0), 0.0)
            bv = a * b_sh + bv
            a = a * a_sh
        hh = a * h + bv
        return hh, (hh[0:1, :] if reverse else hh[SUBLANES - 1:SUBLANES, :])

    def pair_scan(r0, h, reverse):
        if reverse:
            hi_rows, h = tile_scan(r0 + SUBLANES, h, True)
            lo_rows, h = tile_scan(r0, h, True)
        else:
            lo_rows, h = tile_scan(r0, h, False)
            hi_rows, h = tile_scan(r0 + SUBLANES, h, False)
        o_ref[pl.ds(r0, BF16_SUBLANES), :] = jnp.concatenate([lo_rows, hi_rows], axis=0).astype(BF16)
        return h

    n_pairs = n_tiles // 2

    @pl.when(direction == 0)
    def _():
        def body(i, h):
            return pair_scan(pl.multiple_of(i * BF16_SUBLANES, BF16_SUBLANES), h, False)
        carry[...] = lax.fori_loop(0, n_pairs, body, carry[...], unroll=2)

    @pl.when(direction == 1)
    def _():
        def body(i, h):
            return pair_scan(pl.multiple_of((n_pairs - 1 - i) * BF16_SUBLANES, BF16_SUBLANES), h, True)
        carry[...] = lax.fori_loop(0, n_pairs, body, carry[...], unroll=2)

    hfin_ref[...] = carry[...]


def _lru_scan(u, h0, conv_w, conv_b, wg_bf, bg, lam):
    b, s, d = u.shape
    tc = min(TC_SCAN, s)
    n = s // tc
    halo = BF16_SUBLANES
    per = tc // halo
    last_halo = s // halo - 1
    blk = d // N_LRU_BLOCKS

    def chunk_of(di, j):
        return j + di * (n - 1 - 2 * j)

    def u_chunk(di, j):
        return j + di * (n - 1 - j)

    return pl.pallas_call(
        _scan_kernel,
        grid=(b, 2, n),
        in_specs=[
            pl.BlockSpec((None, tc, d), lambda bi, di, j: (bi, u_chunk(di, j), 0)),
            pl.BlockSpec((None, halo, d),
                         lambda bi, di, j: (bi, jnp.maximum(u_chunk(di, j) * per - 1, 0), 0)),
            pl.BlockSpec((None, halo, d),
                         lambda bi, di, j: (bi, jnp.minimum((u_chunk(di, j) + 1) * per, last_halo), 0)),
            pl.BlockSpec((None, None, 1, d), lambda bi, di, j: (di, bi, 0, 0)),
            pl.BlockSpec((CONV_W, d), lambda bi, di, j: (0, 0)),
            pl.BlockSpec((1, d), lambda bi, di, j: (0, 0)),
            pl.BlockSpec((None, N_LRU_BLOCKS, blk, 2 * blk), lambda bi, di, j: (di, 0, 0, 0)),
            pl.BlockSpec((None, 2, d), lambda bi, di, j: (di, 0, 0)),
            pl.BlockSpec((None, 1, d), lambda bi, di, j: (di, 0, 0)),
        ],
        out_specs=[
            pl.BlockSpec((None, None, tc, d), lambda bi, di, j: (di, bi, chunk_of(di, j), 0)),
            pl.BlockSpec((None, None, 1, d), lambda bi, di, j: (di, bi, 0, 0)),
        ],
        out_shape=[jax.ShapeDtypeStruct((2, b, s, d), BF16), jax.ShapeDtypeStruct((2, b, 1, d), F32)],
        scratch_shapes=[
            pltpu.VMEM((tc + 2 * halo, d), F32),
            pltpu.VMEM((n, tc, d), F32),
            pltpu.VMEM((tc, d), F32),
            pltpu.VMEM((tc, d), F32),
            pltpu.VMEM((1, d), F32),
        ],
        compiler_params=_cparams(("arbitrary", "arbitrary", "arbitrary")),
        name="lru_scan",
    )(u, u, u, h0, conv_w, conv_b.reshape(1, d), wg_bf, bg, lam.reshape(2, 1, d))


def _lru_out_kernel(g_ref, hf_ref, hb_ref, x_ref, mod_ref, w_ref, o_ref):
    y = hf_ref[...].astype(F32) + hb_ref[...].astype(F32)
    t = (g_ref[...].astype(F32) * y).astype(BF16)
    o = jnp.dot(t, w_ref[...], preferred_element_type=F32)
    o_ref[...] = x_ref[...] + mod_ref[2:3, :] * o


def _lru_out(g, h, x, mod, w_bf):
    b, s, d = x.shape
    tm = min(TM_PROJ, s)
    row = pl.BlockSpec((None, tm, d), lambda bi, i: (bi, i, 0))
    return pl.pallas_call(
        _lru_out_kernel,
        grid=(b, s // tm),
        in_specs=[
            row,
            pl.BlockSpec((None, None, tm, d), lambda bi, i: (0, bi, i, 0)),
            pl.BlockSpec((None, None, tm, d), lambda bi, i: (1, bi, i, 0)),
            row,
            _mod_spec(mod, d),
            pl.BlockSpec((d, d), lambda bi, i: (0, 0)),
        ],
        out_specs=row,
        out_shape=jax.ShapeDtypeStruct((b, s, d), F32),
        compiler_params=_cparams(("parallel", "parallel")),
        name="lru_out",
    )(g, h, h, x, mod, w_bf)


def _attn_out_kernel(a_ref, x_ref, mod_ref, w_ref, o_ref):
    o = jnp.dot(a_ref[...], w_ref[...], preferred_element_type=F32)
    o_ref[...] = x_ref[...] + mod_ref[2:3, :] * o


def _attn_out(a, x, mod, w_bf):
    b, s, d = x.shape
    tm = min(TM_PROJ, s)
    row = pl.BlockSpec((None, tm, d), lambda bi, i: (bi, i, 0))
    return pl.pallas_call(
        _attn_out_kernel,
        grid=(b, s // tm),
        in_specs=[row, row, _mod_spec(mod, d), pl.BlockSpec((d, d), lambda bi, i: (0, 0))],
        out_specs=row,
        out_shape=jax.ShapeDtypeStruct((b, s, d), F32),
        compiler_params=_cparams(("parallel", "parallel")),
        name="attn_out",
    )(a, x, mod, w_bf)


def _first_hit(vals, target):
    hits = []
    found = None
    for v in vals:
        eq = v == target
        if found is None:
            hits.append(eq)
            found = eq
        else:
            hits.append(jnp.logical_and(eq, jnp.logical_not(found)))
            found = jnp.logical_or(found, eq)
    return hits


def _route_kernel(x_ref, mod_ref, nw_ref, rwt_ref, rb_ref, tri_ref, h_ref, meta_ref, stat_ref):
    n_exp = rwt_ref.shape[0] // 2
    tm = x_ref.shape[0]
    per_group = n_exp // N_EXPERT_GROUPS
    h = _modulate(x_ref[...], nw_ref[...], mod_ref[3:4, :], mod_ref[4:5, :])
    h_hi = h.astype(BF16)
    h_ref[...] = h_hi
    h_lo = (h - h_hi.astype(F32)).astype(BF16)
    nt = (((1,), (1,)), ((), ()))
    by_hi = lax.dot_general(rwt_ref[...], h_hi, nt, preferred_element_type=F32)
    by_lo = lax.dot_general(rwt_ref[0:n_exp, :], h_lo, nt, preferred_element_type=F32)
    logits = by_hi[:n_exp, :] + by_hi[n_exp:, :] + by_lo + rb_ref[...]
    rows = [logits[e:e + 1, :] for e in range(n_exp)]
    mx = functools.reduce(jnp.maximum, rows)
    ex = [jnp.exp(r - mx) for r in rows]
    z = functools.reduce(lambda p, q: p + q, ex)
    probs = [e_ / z for e_ in ex]
    scores = []
    for g in range(N_EXPERT_GROUPS):
        v0, v1, v2, v3 = probs[per_group * g:per_group * (g + 1)]
        hi01, lo01 = jnp.maximum(v0, v1), jnp.minimum(v0, v1)
        hi23, lo23 = jnp.maximum(v2, v3), jnp.minimum(v2, v3)
        top1 = jnp.maximum(hi01, hi23)
        top2 = jnp.maximum(jnp.minimum(hi01, hi23), jnp.maximum(lo01, lo23))
        scores.append(top1 + top2)
    best = functools.reduce(jnp.maximum, scores)
    sel = _first_hit(scores, best)
    masked = [jnp.where(sel[e // per_group], probs[e], -1.0) for e in range(n_exp)]
    m1 = functools.reduce(jnp.maximum, masked)
    is1 = _first_hit(masked, m1)
    masked2 = [jnp.where(is1[e], -2.0, masked[e]) for e in range(n_exp)]
    m2 = functools.reduce(jnp.maximum, masked2)
    is2 = _first_hit(masked2, m2)
    denom = m1 + m2
    gates = [jnp.where(is1[e], m1, jnp.where(is2[e], m2, 0.0)) / denom for e in range(n_exp)]
    zero = jnp.zeros_like(denom)
    for k in range(per_group):
        meta_ref[k:k + 1, :] = functools.reduce(
            lambda p, q: p + q,
            [jnp.where(sel[g], gates[per_group * g + k], 0.0) for g in range(N_EXPERT_GROUPS)])

    onehot = [jnp.where(sel[g], 1.0, 0.0) for g in range(N_EXPERT_GROUPS)]
    onehot_t = jnp.concatenate(onehot + [zero] * (SUBLANES - N_EXPERT_GROUPS), axis=0).astype(BF16)
    earlier = jnp.dot(onehot_t, tri_ref[...], preferred_element_type=F32)
    start = jnp.zeros((1, 1), F32)
    pos = zero
    starts, blocks = [], []
    for g in range(N_EXPERT_GROUPS):
        count = jnp.sum(onehot[g], axis=1, keepdims=True)
        n_blk = jnp.floor((count + (MOE_BLK - 1)) * (1.0 / MOE_BLK))
        pos = pos + onehot[g] * (start + earlier[g:g + 1, :])
        starts.append(start)
        blocks.append(n_blk)
        start = start + n_blk * MOE_BLK
    meta_ref[per_group:per_group + 1, :] = pos
    meta_ref[per_group + 1:SUBLANES, :] = jnp.zeros((SUBLANES - per_group - 1, tm), F32)
    stat_ref[...] = jnp.concatenate([jnp.broadcast_to(v, (1, LANES)) for v in starts + blocks], axis=0)


def _route(x, mod, norm_w, router_w, router_b):
    b, s, d = x.shape
    n_exp = router_w.shape[1]
    assert n_exp // N_EXPERT_GROUPS == 4 and N_EXPERT_GROUPS == 4
    tm = min(TM_MOE, s)
    n_s = s // tm
    row = pl.BlockSpec((None, tm, d), lambda bi, i: (bi, i, 0))
    t_idx = jnp.arange(tm, dtype=jnp.int32)
    tri = (t_idx[:, None] < t_idx[None, :]).astype(BF16)
    rw_t = router_w.T
    rw_hi = rw_t.astype(BF16)
    rw_halves = jnp.concatenate([rw_hi, (rw_t - rw_hi.astype(F32)).astype(BF16)], axis=0)
    return pl.pallas_call(
        _route_kernel,
        grid=(b, n_s),
        in_specs=[
            row,
            _mod_spec(mod, d),
            pl.BlockSpec((1, d), lambda bi, i: (0, 0)),
            pl.BlockSpec((2 * n_exp, d), lambda bi, i: (0, 0)),
            pl.BlockSpec((n_exp, 1), lambda bi, i: (0, 0)),
            pl.BlockSpec((tm, tm), lambda bi, i: (0, 0)),
        ],
        out_specs=[row,
                   pl.BlockSpec((SUBLANES, tm), lambda bi, i: (0, bi * n_s + i)),
                   pl.BlockSpec((None, SUBLANES, LANES), lambda bi, i: (bi * n_s + i, 0, 0))],
        out_shape=[jax.ShapeDtypeStruct((b, s, d), BF16),
                   jax.ShapeDtypeStruct((SUBLANES, b * s), F32),
                   jax.ShapeDtypeStruct((b * n_s, SUBLANES, LANES), F32)],
        compiler_params=_cparams(("parallel", "parallel")),
        name="moe_route",
    )(x, mod, norm_w.reshape(1, d), rw_halves, router_b.reshape(n_exp, 1), tri)


def _moe_kernel(stat_ref, h_ref, mrow_ref, mcol_ref, gcol_ref, x_ref, mod_ref, w1_ref, w3_ref, w2_ref,
                o_ref, xc, gc, yc):
    pair = pl.program_id(0)
    step = pl.program_id(1)
    tm = h_ref.shape[0]
    rows = xc.shape[1]
    per_group = N_EXPERT_GROUPS
    n_gather, n_expert = 2, 2 * N_EXPERT_GROUPS

    for slot in range(2):
        @pl.when(step == slot)
        def _(slot=slot):
            pos_row = mrow_ref[per_group:per_group + 1, :].astype(jnp.int32)
            h = h_ref[...]
            gcol = gcol_ref[...]
            for c in range(rows // MOE_CHUNK):
                r_id = lax.broadcasted_iota(jnp.int32, (MOE_CHUNK, tm), 0) + c * MOE_CHUNK
                perm = jnp.where(pos_row == r_id, 1.0, 0.0).astype(BF16)
                sl = slice(c * MOE_CHUNK, (c + 1) * MOE_CHUNK)
                xc[slot, sl, :] = jnp.dot(perm, h, preferred_element_type=F32).astype(BF16)
                gc[slot, sl, :] = jnp.dot(perm, gcol, preferred_element_type=F32)
            yc[slot] = jnp.zeros((rows, yc.shape[2]), BF16)

    @pl.when(jnp.logical_and(step >= n_gather, step < n_gather + n_expert))
    def _():
        e_step = step - n_gather
        group = e_step // 2
        slot = ((e_step + 1) // 2) % 2
        base = (2 * pair + slot) * SUBLANES
        start = stat_ref[base + group]
        n_blk = stat_ref[base + per_group + group]

        def experts(r0, n_rows):
            xb = xc[slot, pl.ds(r0, n_rows), :]
            g = gc[slot, pl.ds(r0, n_rows), :]
            parts = []
            for k in range(per_group):
                a = jnp.dot(xb, w1_ref[k], preferred_element_type=F32)
                c = jnp.dot(xb, w3_ref[k], preferred_element_type=F32)
                gate = g[:, k:k + 1] + g[:, per_group + k:per_group + k + 1]
                parts.append(((a * _sigmoid(a)) * c * gate).astype(BF16))
            t = jnp.concatenate(parts, axis=1)
            yc[slot, pl.ds(r0, n_rows), :] = jnp.dot(t, w2_ref[...], preferred_element_type=F32).astype(BF16)

        def block_pair(j, carry):
            experts(pl.multiple_of(start + j * (2 * MOE_BLK), MOE_BLK), 2 * MOE_BLK)
            return carry

        lax.fori_loop(0, n_blk // 2, block_pair, 0)

        @pl.when(n_blk % 2 == 1)
        def _():
            experts(pl.multiple_of(start + (n_blk - 1) * MOE_BLK, MOE_BLK), MOE_BLK)

    for slot in range(2):
        @pl.when(step == n_gather + n_expert + slot)
        def _(slot=slot):
            y_sorted = yc[slot]
            for c in range(tm // MOE_CHUNK):
                sl = slice(c * MOE_CHUNK, (c + 1) * MOE_CHUNK)
                pos_col = mcol_ref[sl, per_group:per_group + 1].astype(jnp.int32)
                c_id = lax.broadcasted_iota(jnp.int32, (MOE_CHUNK, rows), 1)
                perm_t = jnp.where(pos_col == c_id, 1.0, 0.0).astype(BF16)
                y = jnp.dot(perm_t, y_sorted, preferred_element_type=F32)
                o_ref[sl, :] = x_ref[sl, :] + mod_ref[5:6, :] * y


def _moe(h, meta, stat, x, mod, w1_bf, w3_bf, w2g):
    b, s, d = x.shape
    n_tok = b * s
    tm = min(TM_MOE, s)
    n_tiles = n_tok // tm
    assert n_tiles % 2 == 0
    rows = tm + N_EXPERT_GROUPS * MOE_BLK
    rows = -(-rows // MOE_CHUNK) * MOE_CHUNK
    per_group, ff = w1_bf.shape[0] // N_EXPERT_GROUPS, w1_bf.shape[2]
    n_gather, n_expert = 2, 2 * N_EXPERT_GROUPS
    n_steps = n_gather + n_expert + 2

    def gather_tile(p, t):
        return 2 * p + jnp.minimum(t, 1)

    def scatter_tile(p, t):
        return 2 * p + jnp.where(t == n_steps - 1, 1, 0)

    def group_of(t):
        return jnp.clip((t - n_gather) // 2, 0, N_EXPERT_GROUPS - 1)

    if mod.shape[0] == 1:
        mod_spec = pl.BlockSpec((None, N_MOD, d), lambda p, t, st: (0, 0, 0))
    else:
        mod_spec = pl.BlockSpec((None, N_MOD, d), lambda p, t, st: ((scatter_tile(p, t) * tm) // s, 0, 0))
    meta_col = meta.T
    gates = meta_col[:, :N_EXPERT_GROUPS]
    g_hi = gates.astype(BF16)
    g_lo = (gates - g_hi.astype(F32)).astype(BF16)
    gcol = jnp.concatenate([g_hi, g_lo, jnp.zeros((n_tok, LANES - 2 * N_EXPERT_GROUPS), BF16)], axis=1)
    stat_i = stat[:, :, 0].astype(jnp.int32).reshape(-1)
    once = pl.Buffered(1)

    grid_spec = pltpu.PrefetchScalarGridSpec(
        num_scalar_prefetch=1,
        grid=(n_tiles // 2, n_steps),
        in_specs=[
            pl.BlockSpec((tm, d), lambda p, t, st: (gather_tile(p, t), 0), pipeline_mode=once),
            pl.BlockSpec((SUBLANES, tm), lambda p, t, st: (0, gather_tile(p, t))),
            pl.BlockSpec((tm, SUBLANES), lambda p, t, st: (scatter_tile(p, t), 0)),
            pl.BlockSpec((tm, LANES), lambda p, t, st: (gather_tile(p, t), 0)),
            pl.BlockSpec((tm, d), lambda p, t, st: (scatter_tile(p, t), 0), pipeline_mode=once),
            mod_spec,
            pl.BlockSpec((per_group, d, ff), lambda p, t, st: (group_of(t), 0, 0)),
            pl.BlockSpec((per_group, d, ff), lambda p, t, st: (group_of(t), 0, 0)),
            pl.BlockSpec((None, per_group * ff, d), lambda p, t, st: (group_of(t), 0, 0)),
        ],
        out_specs=pl.BlockSpec((tm, d), lambda p, t, st: (scatter_tile(p, t), 0), pipeline_mode=once),
        scratch_shapes=[pltpu.VMEM((2, rows, d), BF16), pltpu.VMEM((2, rows, LANES), F32),
                        pltpu.VMEM((2, rows, d), BF16)],
    )
    out = pl.pallas_call(
        _moe_kernel,
        grid_spec=grid_spec,
        out_shape=jax.ShapeDtypeStruct((n_tok, d), F32),
        compiler_params=_cparams(("parallel", "arbitrary"), VMEM_LIMIT_MOE),
        name="moe_experts",
    )(stat_i, h.reshape(n_tok, d), meta, meta_col, gcol, x.reshape(n_tok, d), mod, w1_bf, w3_bf, w2g)
    return out.reshape(b, s, d)


def _group_expert_weights(w1, w3, w2):
    d = w2.shape[2]
    return w1.astype(BF16), w3.astype(BF16), w2.astype(BF16).reshape(N_EXPERT_GROUPS, -1, d)


def _moe_block(x, mod, norm_w, router_w, router_b, w1_bf, w3_bf, w2g):
    shape = x.shape
    if mod.shape[0] == 1:
        x = x.reshape(1, -1, shape[-1])
    h, meta, stat = _route(x, mod, norm_w, router_w, router_b)
    return _moe(h, meta, stat, x, mod, w1_bf, w3_bf, w2g).reshape(shape)


def _head_norm(t, w_row, ones_blockdiag):
    ss = jnp.dot((t * t).astype(BF16), ones_blockdiag, preferred_element_type=F32)
    return t * lax.rsqrt(ss * (1.0 / HEAD_DIM) + EPS) * w_row


def _rope(t, cos, sin_next, sin_prev):
    quarter = HEAD_DIM // 4
    outs = []
    for nb in range(t.shape[1] // LANES):
        tb = t[:, LANES * nb:LANES * (nb + 1)]
        outs.append(tb * cos
                    + pltpu.roll(tb, LANES - quarter, 1) * sin_next
                    + pltpu.roll(tb, quarter, 1) * sin_prev)
    return jnp.concatenate(outs, axis=1)


def _qkv_kernel(x_ref, mod_ref, nw_ref, w_ref, qn_ref, kn_ref, ones_ref, cos_ref, sn_ref, sp_ref,
                q_ref, k_ref, v_ref):
    n_q = q_ref.shape[-1]
    n_kv = k_ref.shape[-1]
    h = _modulate(x_ref[...], nw_ref[...], mod_ref[0:1, :], mod_ref[1:2, :])
    r = jnp.dot(h.astype(BF16), w_ref[...], preferred_element_type=F32)
    cos, sn, sp = cos_ref[...], sn_ref[...], sp_ref[...]
    q = _head_norm(r[:, :n_q], qn_ref[...], ones_ref[...])
    q_ref[...] = (_rope(q, cos, sn, sp) * (math.log2(math.e) / math.sqrt(HEAD_DIM))).astype(BF16)
    k = _head_norm(r[:, n_q:n_q + n_kv], kn_ref[...], ones_ref[0:n_kv, 0:n_kv])
    k_ref[...] = _rope(k, cos, sn, sp).astype(BF16)
    v_ref[...] = r[:, n_q + n_kv:].astype(BF16)


def _kv_ctx_kernel(x_ref, mod_ref, nw_ref, w_ref, kn_ref, ones_ref, k_ref, v_ref):
    n_kv = k_ref.shape[-1]
    h = _modulate(x_ref[...], nw_ref[...], mod_ref[0:1, :], mod_ref[1:2, :])
    r = jnp.dot(h.astype(BF16), w_ref[...], preferred_element_type=F32)
    k_ref[...] = _head_norm(r[:, :n_kv], kn_ref[...], ones_ref[...]).astype(BF16)
    v_ref[...] = r[:, n_kv:].astype(BF16)


def _rope_tables(s):
    half = HEAD_DIM // 2
    quarter = half // 2
    t = jnp.arange(s, dtype=jnp.int32)
    pos_r = (t // GRID_W).astype(F32)[:, None]
    pos_c = (t % GRID_W).astype(F32)[:, None]
    lane = jnp.arange(LANES, dtype=jnp.int32)[None, :]
    in_head = lane % HEAD_DIM
    in_half = in_head % half
    freq = (in_half % quarter).astype(F32)
    inv = ROPE_THETA ** (-freq / quarter)
    ang = jnp.where(in_head < half, pos_r, pos_c) * inv
    cos = jnp.cos(ang)
    sin = jnp.sin(ang)
    first = in_half < quarter
    return cos, jnp.where(first, -sin, 0.0), jnp.where(first, 0.0, sin)


def _ones_blockdiag(n):
    i = jnp.arange(n, dtype=jnp.int32)
    return (i[:, None] // HEAD_DIM == i[None, :] // HEAD_DIM).astype(BF16)


def _qkv(x, mod, norm_w, w_bf, qn_w, kn_w):
    b, s, d = x.shape
    n_kv = N_KV_HEADS * HEAD_DIM
    n_q = w_bf.shape[1] - 2 * n_kv
    tm = min(TM_PROJ, s)
    cos, sn, sp = _rope_tables(s)
    tab = pl.BlockSpec((tm, LANES), lambda i, bi: (i, 0))

    def row(w):
        return pl.BlockSpec((None, tm, w), lambda i, bi: (bi, i, 0))

    def const(shape):
        return pl.BlockSpec(shape, lambda i, bi: (0,) * len(shape))

    return pl.pallas_call(
        _qkv_kernel,
        grid=(s // tm, b),
        in_specs=[
            row(d),
            pl.BlockSpec((None, N_MOD, d), lambda i, bi: (bi, 0, 0)),
            const((1, d)),
            const((d, n_q + 2 * n_kv)),
            const((1, n_q)),
            const((1, n_kv)),
            const((n_q, n_q)),
            tab, tab, tab,
        ],
        out_specs=[row(n_q), row(n_kv), row(n_kv)],
        out_shape=[jax.ShapeDtypeStruct((b, s, n_q), BF16),
                   jax.ShapeDtypeStruct((b, s, n_kv), BF16),
                   jax.ShapeDtypeStruct((b, s, n_kv), BF16)],
        compiler_params=_cparams(("parallel", "parallel")),
        name="attn_qkv",
    )(x, mod, norm_w.reshape(1, d), w_bf,
      jnp.tile(qn_w, n_q // HEAD_DIM).reshape(1, n_q), jnp.tile(kn_w, N_KV_HEADS).reshape(1, n_kv),
      _ones_blockdiag(n_q), cos, sn, sp)


def _kv_ctx(ctx, mod, norm_w, w_kv_bf, kn_w):
    b, s, d = ctx.shape
    n_kv = N_KV_HEADS * HEAD_DIM
    tm = min(TM_PROJ, s)

    def row(w):
        return pl.BlockSpec((None, tm, w), lambda bi, i: (bi, i, 0))

    def const(shape):
        return pl.BlockSpec(shape, lambda bi, i: (0,) * len(shape))

    return pl.pallas_call(
        _kv_ctx_kernel,
        grid=(b, s // tm),
        in_specs=[row(d), _mod_spec(mod, d), const((1, d)), const((d, 2 * n_kv)),
                  const((1, n_kv)), const((n_kv, n_kv))],
        out_specs=[row(n_kv), row(n_kv)],
        out_shape=[jax.ShapeDtypeStruct((b, s, n_kv), BF16), jax.ShapeDtypeStruct((b, s, n_kv), BF16)],
        compiler_params=_cparams(("parallel", "parallel")),
        name="attn_kv_ctx",
    )(ctx, mod, norm_w.reshape(1, d), w_kv_bf, jnp.tile(kn_w, N_KV_HEADS).reshape(1, n_kv),
      _ones_blockdiag(n_kv))


def _attn_kernel(q_ref, k_ref, vt_ref, o_ref, s_scr):
    q = q_ref[...]
    k = k_ref[...]
    vt = vt_ref[...]
    n_keys = k.shape[0]
    n_head_pairs = q.shape[1] // (2 * HEAD_DIM)
    n_units = n_head_pairs * (q.shape[0] // ATTN_QSUB)

    def scores(u):
        rows = slice(ATTN_QSUB * (u // n_head_pairs), ATTN_QSUB * (u // n_head_pairs + 1))
        h0 = 2 * (u % n_head_pairs)
        q2 = jnp.concatenate([q[rows, HEAD_DIM * (h0 + i):HEAD_DIM * (h0 + i + 1)] for i in range(2)], axis=0)
        st = lax.dot_general(k, q2, (((1,), (1,)), ((), ())), preferred_element_type=F32)
        s_scr[u % 2] = st
        slabs = st.reshape(n_keys // ATTN_SLAB, ATTN_SLAB, 2 * ATTN_QSUB)
        return jnp.max(jnp.max(slabs, axis=0), axis=0, keepdims=True)

    def weighted_values(u, m):
        p = jnp.exp2(s_scr[u % 2] - m).astype(BF16)
        ot = jnp.dot(vt, p, preferred_element_type=F32)
        on = ot[:HEAD_DIM, :] / ot[HEAD_DIM:HEAD_DIM + 1, :]
        return [on[:, :ATTN_QSUB], on[:, ATTN_QSUB:]]

    outs = []
    m_next = scores(0)
    for u in range(n_units):
        m = m_next
        if u + 1 < n_units:
            m_next = scores(u + 1)
        outs += weighted_values(u, m)
        if (u + 1) % n_head_pairs == 0:
            rows = slice(ATTN_QSUB * (u // n_head_pairs), ATTN_QSUB * (u // n_head_pairs + 1))
            o_ref[rows, :] = jnp.concatenate(outs, axis=0).T.astype(BF16)
            outs = []


def _attention(q, k, vt):
    assert k.shape[2] % ATTN_SLAB == 0
    b, s, n_q = q.shape
    n_keys = k.shape[2]
    group_w = n_q // N_KV_HEADS
    tq = min(TQ_ATTN, s)
    return pl.pallas_call(
        _attn_kernel,
        grid=(b, N_KV_HEADS, s // tq),
        in_specs=[
            pl.BlockSpec((None, tq, group_w), lambda bi, g, i: (bi, i, g)),
            pl.BlockSpec((None, None, n_keys, HEAD_DIM), lambda bi, g, i: (bi, g, 0, 0)),
            pl.BlockSpec((None, None, vt.shape[2], n_keys), lambda bi, g, i: (bi, g, 0, 0)),
        ],
        out_specs=pl.BlockSpec((None, tq, group_w), lambda bi, g, i: (bi, i, g)),
        out_shape=jax.ShapeDtypeStruct((b, s, n_q), BF16),
        scratch_shapes=[pltpu.VMEM((2, n_keys, 2 * ATTN_QSUB), F32)],
        compiler_params=_cparams(("parallel", "parallel", "parallel")),
        name="attn_core",
    )(q, k, vt)


def _modulations(c, c_ctx, ada_w, ada_b):
    b, d = c.shape
    depth = ada_w.shape[0]
    assert b <= SUBLANES
    cond = jnp.zeros((2 * SUBLANES, d), F32).at[:b].set(c).at[b].set(c_ctx)
    mods = _ada(cond, ada_w, ada_b)
    mod_x = [mods[l, :b].reshape(b, N_MOD, d) for l in range(depth)]
    mod_c = [mods[l, b:b + 1].reshape(1, N_MOD, d) for l in range(depth)]
    return mod_x, mod_c


def _lru_layer(x, ctx, mod_x, mod_c, norm_w, in_w, conv_w, conv_b, ga_w, ga_b, gx_w, gx_b, lam, out_w):
    b, _, d = x.shape
    in_bf = in_w.astype(BF16)
    out_bf = out_w.astype(BF16)
    wg = jnp.concatenate([ga_w, gx_w], axis=-1).astype(BF16)
    bg = jnp.stack([ga_b, gx_b], axis=1)
    g_c, u_c = _inproj(ctx, mod_c, norm_w, in_bf)
    g_x, u_x = _inproj(x, mod_x, norm_w, in_bf)
    h_c, h_c_end = _lru_scan(u_c, jnp.zeros((2, b, 1, d), F32), conv_w, conv_b, wg, bg, lam)
    h_x, _ = _lru_scan(u_x, h_c_end, conv_w, conv_b, wg, bg, lam)
    return _lru_out(g_x, h_x, x, mod_x, out_bf), _lru_out(g_c, h_c, ctx, mod_c, out_bf)


def _attn_layer(x, ctx, mod_x, mod_c, norm_w, qkv_w, qn_w, kn_w, o_w):
    b, s, _ = x.shape
    n_kv = N_KV_HEADS * HEAD_DIM
    qkv_bf = qkv_w.astype(BF16)
    n_q = qkv_bf.shape[1] - 2 * n_kv
    q, k_x, v_x = _qkv(x, mod_x, norm_w, qkv_bf, qn_w, kn_w)
    k_c, v_c = _kv_ctx(ctx, mod_c, norm_w, qkv_bf[:, n_q:], kn_w)
    n_keys = ctx.shape[1] + s
    k_all = jnp.concatenate([k_c, k_x], axis=1).reshape(b, n_keys, N_KV_HEADS, HEAD_DIM)
    v_all = jnp.concatenate([v_c, v_x], axis=1).reshape(b, n_keys, N_KV_HEADS, HEAD_DIM)
    ones_row = jnp.zeros((b, N_KV_HEADS, BF16_SUBLANES, n_keys), BF16).at[:, :, 0].set(1.0)
    vt = jnp.concatenate([v_all.transpose(0, 2, 3, 1), ones_row], axis=2)
    att = _attention(q, k_all.transpose(0, 2, 1, 3), vt)
    return _attn_out(att, x, mod_x, o_w.astype(BF16))


def kernel(x, c, ctx, c_ctx, ada_w, ada_b, norm_mix_w, norm_ffn_w, lru_in_w, lru_conv_w, lru_conv_b, lru_gate_a_w, lru_gate_a_b, lru_gate_x_w, lru_gate_x_b, lru_lambda, lru_out_w, attn_qkv_w, attn_q_norm_w, attn_k_norm_w, attn_o_w, router_w, router_b, moe_w1, moe_w3, moe_w2):
    assert ada_w.shape[0] == 2
    mod_x, mod_c = _modulations(c, c_ctx, ada_w, ada_b)

    x, ctx = _lru_layer(x, ctx, mod_x[0], mod_c[0], norm_mix_w[0], lru_in_w[0], lru_conv_w[0], lru_conv_b[0],
                        lru_gate_a_w[0], lru_gate_a_b[0], lru_gate_x_w[0], lru_gate_x_b[0], lru_lambda[0],
                        lru_out_w[0])
    w1, w3, w2 = _group_expert_weights(moe_w1[0], moe_w3[0], moe_w2[0])
    ctx = _moe_block(ctx, mod_c[0], norm_ffn_w[0], router_w, router_b, w1, w3, w2)
    x = _moe_block(x, mod_x[0], norm_ffn_w[0], router_w, router_b, w1, w3, w2)

    x = _attn_layer(x, ctx, mod_x[1], mod_c[1], norm_mix_w[1], attn_qkv_w[0], attn_q_norm_w[0],
                    attn_k_norm_w[0], attn_o_w[0])
    w1, w3, w2 = _group_expert_weights(moe_w1[1], moe_w3[1], moe_w2[1])
    return _moe_block(x, mod_x[1], norm_ffn_w[1], router_w, router_b, w1, w3, w2)
```

```python
import functools
import math

import jax
import jax.numpy as jnp
from jax import lax
from jax.experimental import pallas as pl
from jax.experimental.pallas import tpu as pltpu

F32 = jnp.float32
BF16 = jnp.bfloat16
HIGHEST = lax.Precision.HIGHEST

GRID_W = 64
N_LRU_BLOCKS = 8
CONV_W = 4
LRU_C = 8.0
HEAD_DIM = 64
N_KV_HEADS = 4
ROPE_THETA = 10000.0
N_EXPERT_GROUPS = 4
N_MOD = 6
EPS = 1e-6

LANES = 128
SUBLANES = 8
BF16_SUBLANES = 16
VMEM_LIMIT = 48 * 1024 * 1024
VMEM_LIMIT_MOE = 56 * 1024 * 1024

TM_PROJ = 512
TC_SCAN = 512
TQ_ATTN = 1024
ATTN_QSUB = 256
ATTN_SLAB = 256
TM_MOE = 1024
MOE_BLK = 128
MOE_CHUNK = 256
ROUTE_POS_LANE = 8


def _cparams(sem, vmem_limit=VMEM_LIMIT):
    return pltpu.CompilerParams(dimension_semantics=sem, vmem_limit_bytes=vmem_limit)


def _sigmoid(x):
    return 1.0 / (1.0 + jnp.exp(-x))


def _gelu_tanh(x):
    c = math.sqrt(2.0 / math.pi)
    return x * (0.5 * (1.0 + jnp.tanh(c * (x + 0.044715 * (x * x * x)))))


def _modulate(xf, g, shift, scale):
    ms = jnp.mean(xf * xf, axis=-1, keepdims=True)
    y = xf * lax.rsqrt(ms + EPS) * g
    return y * (1.0 + scale) + shift


def _mod_spec(mod, d):
    if mod.shape[0] == 1:
        return pl.BlockSpec((None, N_MOD, d), lambda b, i: (0, 0, 0))
    return pl.BlockSpec((None, N_MOD, d), lambda b, i: (b, 0, 0))


def _ada_kernel(cond_ref, w_ref, b_ref, o_ref):
    cnd = cond_ref[...]
    s = cnd * _sigmoid(cnd)
    o_ref[...] = jnp.dot(s, w_ref[...], precision=HIGHEST, preferred_element_type=F32) + b_ref[...]


def _ada(cond, ada_w, ada_b):
    depth, d, n = ada_w.shape
    rows = cond.shape[0]
    tn = 1536
    return pl.pallas_call(
        _ada_kernel,
        grid=(depth, n // tn),
        in_specs=[
            pl.BlockSpec((rows, d), lambda l, j: (0, 0)),
            pl.BlockSpec((None, d, tn), lambda l, j: (l, 0, j)),
            pl.BlockSpec((None, 1, tn), lambda l, j: (l, 0, j)),
        ],
        out_specs=pl.BlockSpec((None, rows, tn), lambda l, j: (l, 0, j)),
        out_shape=jax.ShapeDtypeStruct((depth, rows, n), F32),
        compiler_params=_cparams(("arbitrary", "arbitrary")),
        name="ada",
    )(cond, ada_w, ada_b.reshape(depth, 1, n))


def _inproj_kernel(x_ref, mod_ref, nw_ref, w_ref, g_ref, u_ref):
    d = x_ref.shape[-1]
    h = _modulate(x_ref[...], nw_ref[...], mod_ref[0:1, :], mod_ref[1:2, :])
    r = jnp.dot(h.astype(BF16), w_ref[...], preferred_element_type=F32)
    g_ref[...] = _gelu_tanh(r[:, :d]).astype(BF16)
    u_ref[...] = r[:, d:].astype(BF16)


def _inproj(x, mod, norm_w, w_bf):
    b, s, d = x.shape
    tm = min(TM_PROJ, s)
    row = pl.BlockSpec((None, tm, d), lambda bi, i: (bi, i, 0))
    return pl.pallas_call(
        _inproj_kernel,
        grid=(b, s // tm),
        in_specs=[
            row,
            _mod_spec(mod, d),
            pl.BlockSpec((1, d), lambda bi, i: (0, 0)),
            pl.BlockSpec((d, 2 * d), lambda bi, i: (0, 0)),
        ],
        out_specs=[row, row],
        out_shape=[jax.ShapeDtypeStruct((b, s, d), BF16), jax.ShapeDtypeStruct((b, s, d), BF16)],
        compiler_params=_cparams(("parallel", "parallel")),
        name="lru_inproj",
    )(x, mod, norm_w.reshape(1, d), w_bf)


def _scan_kernel(u_ref, up_ref, un_ref, h0_ref, cw_ref, cb_ref, wg_ref, bg_ref, lam_ref,
                 o_ref, hfin_ref, ubuf, uc_cache, a_s, b_s, carry):
    tc, d = u_ref.shape
    halo = up_ref.shape[0]
    direction = pl.program_id(1)
    j = pl.program_id(2)
    n = pl.num_programs(2)
    chunk = jnp.where(direction == 0, j, n - 1 - j)

    @pl.when(j == 0)
    def _():
        carry[...] = h0_ref[...]

    @pl.when(direction == 0)
    def _():
        ubuf[0:halo, :] = jnp.where(chunk > 0, up_ref[...].astype(F32), 0.0)
        ubuf[halo:halo + tc, :] = u_ref[...].astype(F32)
        ubuf[halo + tc:2 * halo + tc, :] = jnp.where(chunk < n - 1, un_ref[...].astype(F32), 0.0)
        left = CONV_W // 2
        conv = cb_ref[...]
        for k in range(CONV_W):
            conv = conv + ubuf[pl.ds(halo - left + k, tc), :] * cw_ref[k:k + 1, :]
        uc_cache[chunk] = conv

    uc = uc_cache[chunk]

    neg_lam = -lam_ref[...]
    softplus = jnp.maximum(neg_lam, 0.0) + jnp.log1p(jnp.exp(-jnp.abs(neg_lam)))
    decay = -LRU_C * softplus
    ub = uc.astype(BF16)
    blk = d // N_LRU_BLOCKS
    for nb in range(N_LRU_BLOCKS):
        sl = slice(blk * nb, blk * (nb + 1))
        z = jnp.dot(ub[:, sl], wg_ref[nb], preferred_element_type=F32)
        r = _sigmoid(z[:, :blk] + bg_ref[0:1, sl])
        ig = _sigmoid(z[:, blk:] + bg_ref[1:2, sl])
        log_a = decay[:, sl] * r
        a = jnp.exp(log_a)
        a_s[:, sl] = a
        b_s[:, sl] = jnp.sqrt(-jnp.tanh(log_a) * (a * a + 1.0)) * (ig * uc[:, sl])

    row = lax.broadcasted_iota(jnp.int32, (SUBLANES, d), 0)
    n_tiles = tc // SUBLANES

    def tile_scan(r0, h, reverse):
        a = a_s[pl.ds(r0, SUBLANES), :]
        bv = b_s[pl.ds(r0, SUBLANES), :]
        for s in (1, 2, 4):
            if reverse:
                valid = row < SUBLANES - s
                shift = SUBLANES - s
            else:
                valid = row >= s
                shift = s
            a_sh = jnp.where(valid, pltpu.roll(a, shift, 0), 1.0)
            b_sh = jnp.where(valid, pltpu.roll(bv, shift, 0), 0.0)
            bv = a * b_sh + bv
            a = a * a_sh
        hh = a * h + bv
        return hh, (hh[0:1, :] if reverse else hh[SUBLANES - 1:SUBLANES, :])

    def pair_scan(r0, h, reverse):
        if reverse:
            hi_rows, h = tile_scan(r0 + SUBLANES, h, True)
            lo_rows, h = tile_scan(r0, h, True)
        else:
            lo_rows, h = tile_scan(r0, h, False)
            hi_rows, h = tile_scan(r0 + SUBLANES, h, False)
        o_ref[pl.ds(r0, BF16_SUBLANES), :] = jnp.concatenate([lo_rows, hi_rows], axis=0).astype(BF16)
        return h

    n_pairs = n_tiles // 2

    @pl.when(direction == 0)
    def _():
        def body(i, h):
            return pair_scan(pl.multiple_of(i * BF16_SUBLANES, BF16_SUBLANES), h, False)
        carry[...] = lax.fori_loop(0, n_pairs, body, carry[...], unroll=2)

    @pl.when(direction == 1)
    def _():
        def body(i, h):
            return pair_scan(pl.multiple_of((n_pairs - 1 - i) * BF16_SUBLANES, BF16_SUBLANES), h, True)
        carry[...] = lax.fori_loop(0, n_pairs, body, carry[...], unroll=2)

    hfin_ref[...] = carry[...]


def _lru_scan(u, h0, conv_w, conv_b, wg_bf, bg, lam):
    b, s, d = u.shape
    tc = min(TC_SCAN, s)
    n = s // tc
    halo = BF16_SUBLANES
    per = tc // halo
    last_halo = s // halo - 1
    blk = d // N_LRU_BLOCKS

    def chunk_of(di, j):
        return j + di * (n - 1 - 2 * j)

    def u_chunk(di, j):
        return j + di * (n - 1 - j)

    return pl.pallas_call(
        _scan_kernel,
        grid=(b, 2, n),
        in_specs=[
            pl.BlockSpec((None, tc, d), lambda bi, di, j: (bi, u_chunk(di, j), 0)),
            pl.BlockSpec((None, halo, d),
                         lambda bi, di, j: (bi, jnp.maximum(u_chunk(di, j) * per - 1, 0), 0)),
            pl.BlockSpec((None, halo, d),
                         lambda bi, di, j: (bi, jnp.minimum((u_chunk(di, j) + 1) * per, last_halo), 0)),
            pl.BlockSpec((None, None, 1, d), lambda bi, di, j: (di, bi, 0, 0)),
            pl.BlockSpec((CONV_W, d), lambda bi, di, j: (0, 0)),
            pl.BlockSpec((1, d), lambda bi, di, j: (0, 0)),
            pl.BlockSpec((None, N_LRU_BLOCKS, blk, 2 * blk), lambda bi, di, j: (di, 0, 0, 0)),
            pl.BlockSpec((None, 2, d), lambda bi, di, j: (di, 0, 0)),
            pl.BlockSpec((None, 1, d), lambda bi, di, j: (di, 0, 0)),
        ],
        out_specs=[
            pl.BlockSpec((None, None, tc, d), lambda bi, di, j: (di, bi, chunk_of(di, j), 0)),
            pl.BlockSpec((None, None, 1, d), lambda bi, di, j: (di, bi, 0, 0)),
        ],
        out_shape=[jax.ShapeDtypeStruct((2, b, s, d), BF16), jax.ShapeDtypeStruct((2, b, 1, d), F32)],
        scratch_shapes=[
            pltpu.VMEM((tc + 2 * halo, d), F32),
            pltpu.VMEM((n, tc, d), F32),
            pltpu.VMEM((tc, d), F32),
            pltpu.VMEM((tc, d), F32),
            pltpu.VMEM((1, d), F32),
        ],
        compiler_params=_cparams(("arbitrary", "arbitrary", "arbitrary")),
        name="lru_scan",
    )(u, u, u, h0, conv_w, conv_b.reshape(1, d), wg_bf, bg, lam.reshape(2, 1, d))


def _lru_out_kernel(g_ref, hf_ref, hb_ref, x_ref, mod_ref, w_ref, o_ref):
    y = hf_ref[...].astype(F32) + hb_ref[...].astype(F32)
    t = (g_ref[...].astype(F32) * y).astype(BF16)
    o = jnp.dot(t, w_ref[...], preferred_element_type=F32)
    o_ref[...] = x_ref[...] + mod_ref[2:3, :] * o


def _lru_out(g, h, x, mod, w_bf):
    b, s, d = x.shape
    tm = min(TM_PROJ, s)
    row = pl.BlockSpec((None, tm, d), lambda bi, i: (bi, i, 0))
    return pl.pallas_call(
        _lru_out_kernel,
        grid=(b, s // tm),
        in_specs=[
            row,
            pl.BlockSpec((None, None, tm, d), lambda bi, i: (0, bi, i, 0)),
            pl.BlockSpec((None, None, tm, d), lambda bi, i: (1, bi, i, 0)),
            row,
            _mod_spec(mod, d),
            pl.BlockSpec((d, d), lambda bi, i: (0, 0)),
        ],
        out_specs=row,
        out_shape=jax.ShapeDtypeStruct((b, s, d), F32),
        compiler_params=_cparams(("parallel", "parallel")),
        name="lru_out",
    )(g, h, h, x, mod, w_bf)


def _attn_out_kernel(a_ref, x_ref, mod_ref, w_ref, o_ref):
    o = jnp.dot(a_ref[...], w_ref[...], preferred_element_type=F32)
    o_ref[...] = x_ref[...] + mod_ref[2:3, :] * o


def _attn_out(a, x, mod, w_bf):
    b, s, d = x.shape
    tm = min(TM_PROJ, s)
    row = pl.BlockSpec((None, tm, d), lambda bi, i: (bi, i, 0))
    return pl.pallas_call(
        _attn_out_kernel,
        grid=(b, s // tm),
        in_specs=[row, row, _mod_spec(mod, d), pl.BlockSpec((d, d), lambda bi, i: (0, 0))],
        out_specs=row,
        out_shape=jax.ShapeDtypeStruct((b, s, d), F32),
        compiler_params=_cparams(("parallel", "parallel")),
        name="attn_out",
    )(a, x, mod, w_bf)


def _first_hit(vals, target):
    hits = []
    found = None
    for v in vals:
        eq = v == target
        if found is None:
            hits.append(eq)
            found = eq
        else:
            hits.append(jnp.logical_and(eq, jnp.logical_not(found)))
            found = jnp.logical_or(found, eq)
    return hits


def _route_kernel(x_ref, mod_ref, nw_ref, rwt_ref, rb_ref, tri_ref, h_ref, pos_ref, col_ref, gcol_ref, stat_ref):
    n_exp = rwt_ref.shape[0] // 2
    tm = x_ref.shape[0]
    per_group = n_exp // N_EXPERT_GROUPS
    h = _modulate(x_ref[...], nw_ref[...], mod_ref[3:4, :], mod_ref[4:5, :])
    h_hi = h.astype(BF16)
    h_ref[...] = h_hi
    h_lo = (h - h_hi.astype(F32)).astype(BF16)
    nt = (((1,), (1,)), ((), ()))
    by_hi = lax.dot_general(rwt_ref[...], h_hi, nt, preferred_element_type=F32)
    by_lo = lax.dot_general(rwt_ref[0:n_exp, :], h_lo, nt, preferred_element_type=F32)
    logits = by_hi[:n_exp, :] + by_hi[n_exp:, :] + by_lo + rb_ref[...]
    rows = [logits[e:e + 1, :] for e in range(n_exp)]
    mx = functools.reduce(jnp.maximum, rows)
    ex = [jnp.exp(r - mx) for r in rows]
    z = functools.reduce(lambda p, q: p + q, ex)
    probs = [e_ / z for e_ in ex]
    scores = []
    for g in range(N_EXPERT_GROUPS):
        v0, v1, v2, v3 = probs[per_group * g:per_group * (g + 1)]
        hi01, lo01 = jnp.maximum(v0, v1), jnp.minimum(v0, v1)
        hi23, lo23 = jnp.maximum(v2, v3), jnp.minimum(v2, v3)
        top1 = jnp.maximum(hi01, hi23)
        top2 = jnp.maximum(jnp.minimum(hi01, hi23), jnp.maximum(lo01, lo23))
        scores.append(top1 + top2)
    best = functools.reduce(jnp.maximum, scores)
    sel = _first_hit(scores, best)
    masked = [jnp.where(sel[e // per_group], probs[e], -1.0) for e in range(n_exp)]
    m1 = functools.reduce(jnp.maximum, masked)
    is1 = _first_hit(masked, m1)
    masked2 = [jnp.where(is1[e], -2.0, masked[e]) for e in range(n_exp)]
    m2 = functools.reduce(jnp.maximum, masked2)
    is2 = _first_hit(masked2, m2)
    denom = m1 + m2
    gates = [jnp.where(is1[e], m1, jnp.where(is2[e], m2, 0.0)) / denom for e in range(n_exp)]
    zero = jnp.zeros_like(denom)
    chosen = [functools.reduce(lambda p, q: p + q,
                               [jnp.where(sel[g], gates[per_group * g + k], 0.0) for g in range(N_EXPERT_GROUPS)])
              for k in range(per_group)]
    chosen_hi = [v.astype(BF16).astype(F32) for v in chosen]
    chosen_lo = [v - hi for v, hi in zip(chosen, chosen_hi)]

    onehot = [jnp.where(sel[g], 1.0, 0.0) for g in range(N_EXPERT_GROUPS)]
    onehot_t = jnp.concatenate(onehot + [zero] * (SUBLANES - N_EXPERT_GROUPS), axis=0).astype(BF16)
    earlier = jnp.dot(onehot_t, tri_ref[...], preferred_element_type=F32)
    start = jnp.zeros((1, 1), F32)
    pos = zero
    starts, blocks = [], []
    for g in range(N_EXPERT_GROUPS):
        count = jnp.sum(onehot[g], axis=1, keepdims=True)
        n_blk = jnp.floor((count + (MOE_BLK - 1)) * (1.0 / MOE_BLK))
        pos = pos + onehot[g] * (start + earlier[g:g + 1, :])
        starts.append(start)
        blocks.append(n_blk)
        start = start + n_blk * MOE_BLK
    pos_ref[0:1, :] = pos
    pos_ref[1:SUBLANES, :] = jnp.zeros((SUBLANES - 1, tm), F32)
    assert ROUTE_POS_LANE == 2 * per_group
    per_token = jnp.concatenate(chosen_hi + chosen_lo + [pos, jnp.zeros((LANES - ROUTE_POS_LANE - 1, tm), F32)],
                                axis=0)
    per_token_t = per_token.T
    col_ref[...] = per_token_t
    gcol_ref[...] = per_token_t.astype(BF16)
    stat_ref[...] = jnp.concatenate([jnp.broadcast_to(v, (1, LANES)) for v in starts + blocks], axis=0)


def _route(x, mod, norm_w, router_w, router_b):
    b, s, d = x.shape
    n_exp = router_w.shape[1]
    assert n_exp // N_EXPERT_GROUPS == 4 and N_EXPERT_GROUPS == 4
    tm = min(TM_MOE, s)
    n_s = s // tm
    row = pl.BlockSpec((None, tm, d), lambda bi, i: (bi, i, 0))
    t_idx = jnp.arange(tm, dtype=jnp.int32)
    tri = (t_idx[:, None] < t_idx[None, :]).astype(BF16)
    rw_t = router_w.T
    rw_hi = rw_t.astype(BF16)
    rw_halves = jnp.concatenate([rw_hi, (rw_t - rw_hi.astype(F32)).astype(BF16)], axis=0)
    return pl.pallas_call(
        _route_kernel,
        grid=(b, n_s),
        in_specs=[
            row,
            _mod_spec(mod, d),
            pl.BlockSpec((1, d), lambda bi, i: (0, 0)),
            pl.BlockSpec((2 * n_exp, d), lambda bi, i: (0, 0)),
            pl.BlockSpec((n_exp, 1), lambda bi, i: (0, 0)),
            pl.BlockSpec((tm, tm), lambda bi, i: (0, 0)),
        ],
        out_specs=[row,
                   pl.BlockSpec((SUBLANES, tm), lambda bi, i: (0, bi * n_s + i)),
                   pl.BlockSpec((tm, LANES), lambda bi, i: (bi * n_s + i, 0)),
                   pl.BlockSpec((tm, LANES), lambda bi, i: (bi * n_s + i, 0)),
                   pl.BlockSpec((None, SUBLANES, LANES), lambda bi, i: (bi * n_s + i, 0, 0))],
        out_shape=[jax.ShapeDtypeStruct((b, s, d), BF16),
                   jax.ShapeDtypeStruct((SUBLANES, b * s), F32),
                   jax.ShapeDtypeStruct((b * s, LANES), F32),
                   jax.ShapeDtypeStruct((b * s, LANES), BF16),
                   jax.ShapeDtypeStruct((b * n_s, SUBLANES, LANES), F32)],
        compiler_params=_cparams(("parallel", "parallel")),
        name="moe_route",
    )(x, mod, norm_w.reshape(1, d), rw_halves, router_b.reshape(n_exp, 1), tri)


def _moe_kernel(stat_ref, h_ref, mrow_ref, mcol_ref, gcol_ref, x_ref, mod_ref, w1_ref, w3_ref, w2_ref,
                o_ref, xc, gc, yc):
    pair = pl.program_id(0)
    step = pl.program_id(1)
    tm = h_ref.shape[0]
    rows = xc.shape[1]
    per_group = N_EXPERT_GROUPS
    n_gather, n_expert = 2, 2 * N_EXPERT_GROUPS

    for slot in range(2):
        @pl.when(step == slot)
        def _(slot=slot):
            pos_row = mrow_ref[0:1, :].astype(jnp.int32)
            h = h_ref[...]
            gcol = gcol_ref[...]
            for c in range(rows // MOE_CHUNK):
                r_id = lax.broadcasted_iota(jnp.int32, (MOE_CHUNK, tm), 0) + c * MOE_CHUNK
                perm = jnp.where(pos_row == r_id, 1.0, 0.0).astype(BF16)
                sl = slice(c * MOE_CHUNK, (c + 1) * MOE_CHUNK)
                xc[slot, sl, :] = jnp.dot(perm, h, preferred_element_type=F32).astype(BF16)
                gc[slot, sl, :] = jnp.dot(perm, gcol, preferred_element_type=F32)
            yc[slot] = jnp.zeros((rows, yc.shape[2]), BF16)

    @pl.when(jnp.logical_and(step >= n_gather, step < n_gather + n_expert))
    def _():
        e_step = step - n_gather
        group = e_step // 2
        slot = ((e_step + 1) // 2) % 2
        base = (2 * pair + slot) * SUBLANES
        start = stat_ref[base + group]
        n_blk = stat_ref[base + per_group + group]

        def experts(r0, n_rows):
            xb = xc[slot, pl.ds(r0, n_rows), :]
            g = gc[slot, pl.ds(r0, n_rows), :]
            parts = []
            for k in range(per_group):
                a = jnp.dot(xb, w1_ref[k], preferred_element_type=F32)
                c = jnp.dot(xb, w3_ref[k], preferred_element_type=F32)
                gate = g[:, k:k + 1] + g[:, per_group + k:per_group + k + 1]
                parts.append(((a * _sigmoid(a)) * c * gate).astype(BF16))
            t = jnp.concatenate(parts, axis=1)
            yc[slot, pl.ds(r0, n_rows), :] = jnp.dot(t, w2_ref[...], preferred_element_type=F32).astype(BF16)

        def block_pair(j, carry):
            experts(pl.multiple_of(start + j * (2 * MOE_BLK), MOE_BLK), 2 * MOE_BLK)
            return carry

        lax.fori_loop(0, n_blk // 2, block_pair, 0)

        @pl.when(n_blk % 2 == 1)
        def _():
            experts(pl.multiple_of(start + (n_blk - 1) * MOE_BLK, MOE_BLK), MOE_BLK)

    for slot in range(2):
        @pl.when(step == n_gather + n_expert + slot)
        def _(slot=slot):
            y_sorted = yc[slot]
            for c in range(tm // MOE_CHUNK):
                sl = slice(c * MOE_CHUNK, (c + 1) * MOE_CHUNK)
                pos_col = mcol_ref[sl, ROUTE_POS_LANE:ROUTE_POS_LANE + 1].astype(jnp.int32)
                c_id = lax.broadcasted_iota(jnp.int32, (MOE_CHUNK, rows), 1)
                perm_t = jnp.where(pos_col == c_id, 1.0, 0.0).astype(BF16)
                y = jnp.dot(perm_t, y_sorted, preferred_element_type=F32)
                o_ref[sl, :] = x_ref[sl, :] + mod_ref[5:6, :] * y


def _moe(h, pos_rows, col, gcol, stat, x, mod, w1_bf, w3_bf, w2g):
    b, s, d = x.shape
    n_tok = b * s
    tm = min(TM_MOE, s)
    n_tiles = n_tok // tm
    assert n_tiles % 2 == 0
    rows = tm + N_EXPERT_GROUPS * MOE_BLK
    rows = -(-rows // MOE_CHUNK) * MOE_CHUNK
    per_group, ff = w1_bf.shape[0] // N_EXPERT_GROUPS, w1_bf.shape[2]
    n_gather, n_expert = 2, 2 * N_EXPERT_GROUPS
    n_steps = n_gather + n_expert + 2

    def gather_tile(p, t):
        return 2 * p + jnp.minimum(t, 1)

    def scatter_tile(p, t):
        return 2 * p + jnp.where(t == n_steps - 1, 1, 0)

    def group_of(t):
        return jnp.clip((t - n_gather) // 2, 0, N_EXPERT_GROUPS - 1)

    if mod.shape[0] == 1:
        mod_spec = pl.BlockSpec((None, N_MOD, d), lambda p, t, st: (0, 0, 0))
    else:
        mod_spec = pl.BlockSpec((None, N_MOD, d), lambda p, t, st: ((scatter_tile(p, t) * tm) // s, 0, 0))
    stat_i = stat[:, :, 0].astype(jnp.int32).reshape(-1)
    once = pl.Buffered(1)

    grid_spec = pltpu.PrefetchScalarGridSpec(
        num_scalar_prefetch=1,
        grid=(n_tiles // 2, n_steps),
        in_specs=[
            pl.BlockSpec((tm, d), lambda p, t, st: (gather_tile(p, t), 0), pipeline_mode=once),
            pl.BlockSpec((SUBLANES, tm), lambda p, t, st: (0, gather_tile(p, t))),
            pl.BlockSpec((tm, LANES), lambda p, t, st: (scatter_tile(p, t), 0)),
            pl.BlockSpec((tm, LANES), lambda p, t, st: (gather_tile(p, t), 0)),
            pl.BlockSpec((tm, d), lambda p, t, st: (scatter_tile(p, t), 0), pipeline_mode=once),
            mod_spec,
            pl.BlockSpec((per_group, d, ff), lambda p, t, st: (group_of(t), 0, 0)),
            pl.BlockSpec((per_group, d, ff), lambda p, t, st: (group_of(t), 0, 0)),
            pl.BlockSpec((None, per_group * ff, d), lambda p, t, st: (group_of(t), 0, 0)),
        ],
        out_specs=pl.BlockSpec((tm, d), lambda p, t, st: (scatter_tile(p, t), 0), pipeline_mode=once),
        scratch_shapes=[pltpu.VMEM((2, rows, d), BF16), pltpu.VMEM((2, rows, LANES), F32),
                        pltpu.VMEM((2, rows, d), BF16)],
    )
    out = pl.pallas_call(
        _moe_kernel,
        grid_spec=grid_spec,
        out_shape=jax.ShapeDtypeStruct((n_tok, d), F32),
        compiler_params=_cparams(("parallel", "arbitrary"), VMEM_LIMIT_MOE),
        name="moe_experts",
    )(stat_i, h.reshape(n_tok, d), pos_rows, col, gcol, x.reshape(n_tok, d), mod, w1_bf, w3_bf, w2g)
    return out.reshape(b, s, d)


def _group_expert_weights(w1, w3, w2):
    d = w2.shape[2]
    return w1.astype(BF16), w3.astype(BF16), w2.astype(BF16).reshape(N_EXPERT_GROUPS, -1, d)


def _moe_block(x, mod, norm_w, router_w, router_b, w1_bf, w3_bf, w2g):
    shape = x.shape
    if mod.shape[0] == 1:
        x = x.reshape(1, -1, shape[-1])
    h, pos_rows, col, gcol, stat = _route(x, mod, norm_w, router_w, router_b)
    return _moe(h, pos_rows, col, gcol, stat, x, mod, w1_bf, w3_bf, w2g).reshape(shape)


def _head_norm(t, w_row, ones_blockdiag):
    ss = jnp.dot((t * t).astype(BF16), ones_blockdiag, preferred_element_type=F32)
    return t * lax.rsqrt(ss * (1.0 / HEAD_DIM) + EPS) * w_row


def _rope(t, cos, sin_next, sin_prev):
    quarter = HEAD_DIM // 4
    outs = []
    for nb in range(t.shape[1] // LANES):
        tb = t[:, LANES * nb:LANES * (nb + 1)]
        outs.append(tb * cos
                    + pltpu.roll(tb, LANES - quarter, 1) * sin_next
                    + pltpu.roll(tb, quarter, 1) * sin_prev)
    return jnp.concatenate(outs, axis=1)


def _store_kv(k, v, k_ref, vt_ref):
    tm = k.shape[0]
    vt = v.T
    pad = jnp.concatenate([jnp.ones((1, tm), F32), jnp.zeros((BF16_SUBLANES - 1, tm), F32)], axis=0)
    for h in range(N_KV_HEADS):
        k_ref[h] = k[:, HEAD_DIM * h:HEAD_DIM * (h + 1)].astype(BF16)
        vt_ref[h] = jnp.concatenate([vt[HEAD_DIM * h:HEAD_DIM * (h + 1), :], pad], axis=0).astype(BF16)


def _qkv_kernel(x_ref, mod_ref, nw_ref, w_ref, qn_ref, kn_ref, ones_ref, cos_ref, sn_ref, sp_ref,
                k_all_ref, vt_all_ref, q_ref, k_ref, vt_ref):
    del k_all_ref, vt_all_ref
    n_q = q_ref.shape[-1]
    n_kv = N_KV_HEADS * HEAD_DIM
    h = _modulate(x_ref[...], nw_ref[...], mod_ref[0:1, :], mod_ref[1:2, :])
    r = jnp.dot(h.astype(BF16), w_ref[...], preferred_element_type=F32)
    cos, sn, sp = cos_ref[...], sn_ref[...], sp_ref[...]
    q = _head_norm(r[:, :n_q], qn_ref[...], ones_ref[...])
    q_ref[...] = (_rope(q, cos, sn, sp) * (math.log2(math.e) / math.sqrt(HEAD_DIM))).astype(BF16)
    k = _head_norm(r[:, n_q:n_q + n_kv], kn_ref[...], ones_ref[0:n_kv, 0:n_kv])
    _store_kv(_rope(k, cos, sn, sp), r[:, n_q + n_kv:], k_ref, vt_ref)


def _kv_ctx_kernel(x_ref, mod_ref, nw_ref, w_ref, kn_ref, ones_ref, k_ref, vt_ref):
    n_kv = N_KV_HEADS * HEAD_DIM
    h = _modulate(x_ref[...], nw_ref[...], mod_ref[0:1, :], mod_ref[1:2, :])
    r = jnp.dot(h.astype(BF16), w_ref[...], preferred_element_type=F32)
    _store_kv(_head_norm(r[:, :n_kv], kn_ref[...], ones_ref[...]), r[:, n_kv:], k_ref, vt_ref)


def _rope_tables(s):
    half = HEAD_DIM // 2
    quarter = half // 2
    t = jnp.arange(s, dtype=jnp.int32)
    pos_r = (t // GRID_W).astype(F32)[:, None]
    pos_c = (t % GRID_W).astype(F32)[:, None]
    lane = jnp.arange(LANES, dtype=jnp.int32)[None, :]
    in_head = lane % HEAD_DIM
    in_half = in_head % half
    freq = (in_half % quarter).astype(F32)
    inv = ROPE_THETA ** (-freq / quarter)
    ang = jnp.where(in_head < half, pos_r, pos_c) * inv
    cos = jnp.cos(ang)
    sin = jnp.sin(ang)
    first = in_half < quarter
    return cos, jnp.where(first, -sin, 0.0), jnp.where(first, 0.0, sin)


def _ones_blockdiag(n):
    i = jnp.arange(n, dtype=jnp.int32)
    return (i[:, None] // HEAD_DIM == i[None, :] // HEAD_DIM).astype(BF16)


def _qkv(x, mod, norm_w, w_bf, qn_w, kn_w, k_all, vt_all, n_ctx):
    b, s, d = x.shape
    n_kv = N_KV_HEADS * HEAD_DIM
    n_q = w_bf.shape[1] - 2 * n_kv
    tm = n_ctx
    assert s % tm == 0 and tm % LANES == 0
    cos, sn, sp = _rope_tables(s)
    tab = pl.BlockSpec((tm, LANES), lambda i, bi: (i, 0))

    def row(w):
        return pl.BlockSpec((None, tm, w), lambda i, bi: (bi, i, 0))

    def const(shape):
        return pl.BlockSpec(shape, lambda i, bi: (0,) * len(shape))

    return pl.pallas_call(
        _qkv_kernel,
        grid=(s // tm, b),
        in_specs=[
            row(d),
            pl.BlockSpec((None, N_MOD, d), lambda i, bi: (bi, 0, 0)),
            const((1, d)),
            const((d, n_q + 2 * n_kv)),
            const((1, n_q)),
            const((1, n_kv)),
            const((n_q, n_q)),
            tab, tab, tab,
            pl.BlockSpec(memory_space=pl.ANY),
            pl.BlockSpec(memory_space=pl.ANY),
        ],
        out_specs=[row(n_q),
                   pl.BlockSpec((None, N_KV_HEADS, tm, HEAD_DIM), lambda i, bi: (bi, 0, i + 1, 0)),
                   pl.BlockSpec((None, N_KV_HEADS, vt_all.shape[2], tm), lambda i, bi: (bi, 0, 0, i + 1))],
        out_shape=[jax.ShapeDtypeStruct((b, s, n_q), BF16),
                   jax.ShapeDtypeStruct(k_all.shape, BF16),
                   jax.ShapeDtypeStruct(vt_all.shape, BF16)],
        input_output_aliases={10: 1, 11: 2},
        compiler_params=_cparams(("parallel", "parallel")),
        name="attn_qkv",
    )(x, mod, norm_w.reshape(1, d), w_bf,
      jnp.tile(qn_w, n_q // HEAD_DIM).reshape(1, n_q), jnp.tile(kn_w, N_KV_HEADS).reshape(1, n_kv),
      _ones_blockdiag(n_q), cos, sn, sp, k_all, vt_all)


def _kv_ctx(ctx, mod, norm_w, w_kv_bf, kn_w, n_keys):
    b, s, d = ctx.shape
    n_kv = N_KV_HEADS * HEAD_DIM
    v_rows = HEAD_DIM + BF16_SUBLANES

    def const(shape):
        return pl.BlockSpec(shape, lambda bi: (0,) * len(shape))

    return pl.pallas_call(
        _kv_ctx_kernel,
        grid=(b,),
        in_specs=[pl.BlockSpec((None, s, d), lambda bi: (bi, 0, 0)),
                  pl.BlockSpec((None, N_MOD, d), lambda bi: (0, 0, 0)),
                  const((1, d)), const((d, 2 * n_kv)), const((1, n_kv)), const((n_kv, n_kv))],
        out_specs=[pl.BlockSpec((None, N_KV_HEADS, s, HEAD_DIM), lambda bi: (bi, 0, 0, 0)),
                   pl.BlockSpec((None, N_KV_HEADS, v_rows, s), lambda bi: (bi, 0, 0, 0))],
        out_shape=[jax.ShapeDtypeStruct((b, N_KV_HEADS, n_keys, HEAD_DIM), BF16),
                   jax.ShapeDtypeStruct((b, N_KV_HEADS, v_rows, n_keys), BF16)],
        compiler_params=_cparams(("parallel",)),
        name="attn_kv_ctx",
    )(ctx, mod, norm_w.reshape(1, d), w_kv_bf, jnp.tile(kn_w, N_KV_HEADS).reshape(1, n_kv),
      _ones_blockdiag(n_kv))


def _attn_kernel(q_ref, k_ref, vt_ref, o_ref, s_scr):
    q = q_ref[...]
    k = k_ref[...]
    vt = vt_ref[...]
    n_keys = k.shape[0]
    n_head_pairs = q.shape[1] // (2 * HEAD_DIM)
    n_units = n_head_pairs * (q.shape[0] // ATTN_QSUB)

    def scores(u):
        rows = slice(ATTN_QSUB * (u // n_head_pairs), ATTN_QSUB * (u // n_head_pairs + 1))
        h0 = 2 * (u % n_head_pairs)
        q2 = jnp.concatenate([q[rows, HEAD_DIM * (h0 + i):HEAD_DIM * (h0 + i + 1)] for i in range(2)], axis=0)
        st = lax.dot_general(k, q2, (((1,), (1,)), ((), ())), preferred_element_type=F32)
        s_scr[u % 2] = st
        slabs = st.reshape(n_keys // ATTN_SLAB, ATTN_SLAB, 2 * ATTN_QSUB)
        return jnp.max(jnp.max(slabs, axis=0), axis=0, keepdims=True)

    def weighted_values(u, m):
        p = jnp.exp2(s_scr[u % 2] - m).astype(BF16)
        ot = jnp.dot(vt, p, preferred_element_type=F32)
        on = ot[:HEAD_DIM, :] / ot[HEAD_DIM:HEAD_DIM + 1, :]
        return [on[:, :ATTN_QSUB], on[:, ATTN_QSUB:]]

    outs = []
    m_next = scores(0)
    for u in range(n_units):
        m = m_next
        if u + 1 < n_units:
            m_next = scores(u + 1)
        outs += weighted_values(u, m)
        if (u + 1) % n_head_pairs == 0:
            rows = slice(ATTN_QSUB * (u // n_head_pairs), ATTN_QSUB * (u // n_head_pairs + 1))
            o_ref[rows, :] = jnp.concatenate(outs, axis=0).T.astype(BF16)
            outs = []


def _attention(q, k, vt):
    assert k.shape[2] % ATTN_SLAB == 0
    b, s, n_q = q.shape
    n_keys = k.shape[2]
    group_w = n_q // N_KV_HEADS
    tq = min(TQ_ATTN, s)
    return pl.pallas_call(
        _attn_kernel,
        grid=(b, N_KV_HEADS, s // tq),
        in_specs=[
            pl.BlockSpec((None, tq, group_w), lambda bi, g, i: (bi, i, g)),
            pl.BlockSpec((None, None, n_keys, HEAD_DIM), lambda bi, g, i: (bi, g, 0, 0)),
            pl.BlockSpec((None, None, vt.shape[2], n_keys), lambda bi, g, i: (bi, g, 0, 0)),
        ],
        out_specs=pl.BlockSpec((None, tq, group_w), lambda bi, g, i: (bi, i, g)),
        out_shape=jax.ShapeDtypeStruct((b, s, n_q), BF16),
        scratch_shapes=[pltpu.VMEM((2, n_keys, 2 * ATTN_QSUB), F32)],
        compiler_params=_cparams(("parallel", "parallel", "parallel")),
        name="attn_core",
    )(q, k, vt)


def _modulations(c, c_ctx, ada_w, ada_b):
    b, d = c.shape
    depth = ada_w.shape[0]
    assert b <= SUBLANES
    cond = jnp.zeros((2 * SUBLANES, d), F32).at[:b].set(c).at[b].set(c_ctx)
    mods = _ada(cond, ada_w, ada_b)
    mod_x = [mods[l, :b].reshape(b, N_MOD, d) for l in range(depth)]
    mod_c = [mods[l, b:b + 1].reshape(1, N_MOD, d) for l in range(depth)]
    return mod_x, mod_c


def _lru_layer(x, ctx, mod_x, mod_c, norm_w, in_w, conv_w, conv_b, ga_w, ga_b, gx_w, gx_b, lam, out_w):
    b, _, d = x.shape
    in_bf = in_w.astype(BF16)
    out_bf = out_w.astype(BF16)
    wg = jnp.concatenate([ga_w, gx_w], axis=-1).astype(BF16)
    bg = jnp.stack([ga_b, gx_b], axis=1)
    g_c, u_c = _inproj(ctx, mod_c, norm_w, in_bf)
    g_x, u_x = _inproj(x, mod_x, norm_w, in_bf)
    h_c, h_c_end = _lru_scan(u_c, jnp.zeros((2, b, 1, d), F32), conv_w, conv_b, wg, bg, lam)
    h_x, _ = _lru_scan(u_x, h_c_end, conv_w, conv_b, wg, bg, lam)
    return _lru_out(g_x, h_x, x, mod_x, out_bf), _lru_out(g_c, h_c, ctx, mod_c, out_bf)


def _attn_layer(x, ctx, mod_x, mod_c, norm_w, qkv_w, qn_w, kn_w, o_w):
    b, s, _ = x.shape
    n_kv = N_KV_HEADS * HEAD_DIM
    qkv_bf = qkv_w.astype(BF16)
    n_q = qkv_bf.shape[1] - 2 * n_kv
    n_ctx = ctx.shape[1]
    k_all, vt_all = _kv_ctx(ctx, mod_c, norm_w, qkv_bf[:, n_q:], kn_w, n_ctx + s)
    q, k_all, vt_all = _qkv(x, mod_x, norm_w, qkv_bf, qn_w, kn_w, k_all, vt_all, n_ctx)
    att = _attention(q, k_all, vt_all)
    return _attn_out(att, x, mod_x, o_w.astype(BF16))


def kernel(x, c, ctx, c_ctx, ada_w, ada_b, norm_mix_w, norm_ffn_w, lru_in_w, lru_conv_w, lru_conv_b, lru_gate_a_w, lru_gate_a_b, lru_gate_x_w, lru_gate_x_b, lru_lambda, lru_out_w, attn_qkv_w, attn_q_norm_w, attn_k_norm_w, attn_o_w, router_w, router_b, moe_w1, moe_w3, moe_w2):
    assert ada_w.shape[0] == 2
    mod_x, mod_c = _modulations(c, c_ctx, ada_w, ada_b)

    x, ctx = _lru_layer(x, ctx, mod_x[0], mod_c[0], norm_mix_w[0], lru_in_w[0], lru_conv_w[0], lru_conv_b[0],
                        lru_gate_a_w[0], lru_gate_a_b[0], lru_gate_x_w[0], lru_gate_x_b[0], lru_lambda[0],
                        lru_out_w[0])
    w1, w3, w2 = _group_expert_weights(moe_w1[0], moe_w3[0], moe_w2[0])
    ctx = _moe_block(ctx, mod_c[0], norm_ffn_w[0], router_w, router_b, w1, w3, w2)
    x = _moe_block(x, mod_x[0], norm_ffn_w[0], router_w, router_b, w1, w3, w2)

    x = _attn_layer(x, ctx, mod_x[1], mod_c[1], norm_mix_w[1], attn_qkv_w[0], attn_q_norm_w[0],
                    attn_k_norm_w[0], attn_o_w[0])
    w1, w3, w2 = _group_expert_weights(moe_w1[1], moe_w3[1], moe_w2[1])
    return _moe_block(x, mod_x[1], norm_ffn_w[1], router_w, router_b, w1, w3, w2)
```

```python
import functools
import math

import jax
import jax.numpy as jnp
import numpy as np
from jax import lax
from jax.experimental import pallas as pl
from jax.experimental.pallas import tpu as pltpu

F32 = jnp.float32
BF16 = jnp.bfloat16
HIGHEST = lax.Precision.HIGHEST

GRID_W = 64
N_LRU_BLOCKS = 8
CONV_W = 4
LRU_C = 8.0
HEAD_DIM = 64
N_KV_HEADS = 4
ROPE_THETA = 10000.0
N_EXPERT_GROUPS = 4
N_MOD = 6
EPS = 1e-6

LANES = 128
SUBLANES = 8
BF16_SUBLANES = 16
VMEM_LIMIT = 48 * 1024 * 1024
VMEM_LIMIT_MOE = 56 * 1024 * 1024

TM_PROJ = 1024
TC_SCAN = 1024
TQ_ATTN = 1024
ATTN_QSUB = 256
ATTN_SLAB = 256
TM_MOE = 1024
MOE_BLK = 128
MOE_CHUNK = 256
ROUTE_POS_LANE = 8


def _cparams(sem, vmem_limit=VMEM_LIMIT):
    return pltpu.CompilerParams(dimension_semantics=sem, vmem_limit_bytes=vmem_limit)


def _sigmoid(x):
    return 1.0 / (1.0 + jnp.exp(-x))


def _gelu_tanh(x):
    c = math.sqrt(2.0 / math.pi)
    return x * (0.5 * (1.0 + jnp.tanh(c * (x + 0.044715 * (x * x * x)))))


def _modulate(xf, g, shift, scale):
    ms = jnp.mean(xf * xf, axis=-1, keepdims=True)
    y = xf * lax.rsqrt(ms + EPS) * g
    return y * (1.0 + scale) + shift


def _mod_spec(mod, d):
    if mod.shape[0] == 1:
        return pl.BlockSpec((None, N_MOD, d), lambda b, i: (0, 0, 0))
    return pl.BlockSpec((None, N_MOD, d), lambda b, i: (b, 0, 0))


def _ada_kernel(cond_ref, w_ref, b_ref, o_ref):
    cnd = cond_ref[...]
    s = cnd * _sigmoid(cnd)
    o_ref[...] = jnp.dot(s, w_ref[...], precision=HIGHEST, preferred_element_type=F32) + b_ref[...]


def _ada(cond, ada_w, ada_b):
    depth, d, n = ada_w.shape
    rows = cond.shape[0]
    tn = 1536
    return pl.pallas_call(
        _ada_kernel,
        grid=(depth, n // tn),
        in_specs=[
            pl.BlockSpec((rows, d), lambda l, j: (0, 0)),
            pl.BlockSpec((None, d, tn), lambda l, j: (l, 0, j)),
            pl.BlockSpec((None, 1, tn), lambda l, j: (l, 0, j)),
        ],
        out_specs=pl.BlockSpec((None, rows, tn), lambda l, j: (l, 0, j)),
        out_shape=jax.ShapeDtypeStruct((depth, rows, n), F32),
        compiler_params=_cparams(("arbitrary", "arbitrary")),
        name="ada",
    )(cond, ada_w, ada_b.reshape(depth, 1, n))


def _inproj_kernel(x_ref, mod_ref, nw_ref, w_ref, g_ref, u_ref):
    d = x_ref.shape[-1]
    h = _modulate(x_ref[...], nw_ref[...], mod_ref[0:1, :], mod_ref[1:2, :])
    r = jnp.dot(h.astype(BF16), w_ref[...], preferred_element_type=F32)
    g_ref[...] = _gelu_tanh(r[:, :d]).astype(BF16)
    u_ref[...] = r[:, d:].astype(BF16)


def _inproj(x, mod, norm_w, w_bf):
    b, s, d = x.shape
    tm = min(TM_PROJ, s)
    row = pl.BlockSpec((None, tm, d), lambda bi, i: (bi, i, 0))
    return pl.pallas_call(
        _inproj_kernel,
        grid=(b, s // tm),
        in_specs=[
            row,
            _mod_spec(mod, d),
            pl.BlockSpec((1, d), lambda bi, i: (0, 0)),
            pl.BlockSpec((d, 2 * d), lambda bi, i: (0, 0)),
        ],
        out_specs=[row, row],
        out_shape=[jax.ShapeDtypeStruct((b, s, d), BF16), jax.ShapeDtypeStruct((b, s, d), BF16)],
        compiler_params=_cparams(("parallel", "parallel")),
        name="lru_inproj",
    )(x, mod, norm_w.reshape(1, d), w_bf)


def _scan_kernel(u_ref, up_ref, un_ref, h0_ref, cw_ref, cb_ref, wg_ref, bg_ref, lam_ref,
                 o_ref, hfin_ref, ubuf, uc_cache, a_s, b_s, carry):
    tc, d = u_ref.shape
    halo = up_ref.shape[0]
    direction = pl.program_id(1)
    j = pl.program_id(2)
    n = pl.num_programs(2)
    chunk = jnp.where(direction == 0, j, n - 1 - j)

    @pl.when(j == 0)
    def _():
        carry[...] = h0_ref[...]

    @pl.when(direction == 0)
    def _():
        ubuf[0:halo, :] = jnp.where(chunk > 0, up_ref[...].astype(F32), 0.0)
        ubuf[halo:halo + tc, :] = u_ref[...].astype(F32)
        ubuf[halo + tc:2 * halo + tc, :] = jnp.where(chunk < n - 1, un_ref[...].astype(F32), 0.0)
        left = CONV_W // 2
        conv = cb_ref[...]
        for k in range(CONV_W):
            conv = conv + ubuf[pl.ds(halo - left + k, tc), :] * cw_ref[k:k + 1, :]
        uc_cache[chunk] = conv

    uc = uc_cache[chunk]

    neg_lam = -lam_ref[...]
    softplus = jnp.maximum(neg_lam, 0.0) + jnp.log1p(jnp.exp(-jnp.abs(neg_lam)))
    decay = -LRU_C * softplus
    ub = uc.astype(BF16)
    blk = d // N_LRU_BLOCKS
    for nb in range(N_LRU_BLOCKS):
        sl = slice(blk * nb, blk * (nb + 1))
        z = jnp.dot(ub[:, sl], wg_ref[nb], preferred_element_type=F32)
        r = _sigmoid(z[:, :blk] + bg_ref[0:1, sl])
        ig = _sigmoid(z[:, blk:] + bg_ref[1:2, sl])
        log_a = decay[:, sl] * r
        a = jnp.exp(log_a)
        a_s[:, sl] = a
        b_s[:, sl] = jnp.sqrt(-jnp.tanh(log_a) * (a * a + 1.0)) * (ig * uc[:, sl])

    row = lax.broadcasted_iota(jnp.int32, (SUBLANES, d), 0)
    n_tiles = tc // SUBLANES

    def tile_scan(r0, h, reverse):
        a = a_s[pl.ds(r0, SUBLANES), :]
        bv = b_s[pl.ds(r0, SUBLANES), :]
        for s in (1, 2, 4):
            if reverse:
                valid = row < SUBLANES - s
                shift = SUBLANES - s
            else:
                valid = row >= s
                shift = s
            a_sh = jnp.where(valid, pltpu.roll(a, shift, 0), 1.0)
            b_sh = jnp.where(valid, pltpu.roll(bv, shift, 0), 0.0)
            bv = a * b_sh + bv
            a = a * a_sh
        hh = a * h + bv
        return hh, (hh[0:1, :] if reverse else hh[SUBLANES - 1:SUBLANES, :])

    def pair_scan(r0, h, reverse):
        if reverse:
            hi_rows, h = tile_scan(r0 + SUBLANES, h, True)
            lo_rows, h = tile_scan(r0, h, True)
        else:
            lo_rows, h = tile_scan(r0, h, False)
            hi_rows, h = tile_scan(r0 + SUBLANES, h, False)
        o_ref[pl.ds(r0, BF16_SUBLANES), :] = jnp.concatenate([lo_rows, hi_rows], axis=0).astype(BF16)
        return h

    n_pairs = n_tiles // 2

    @pl.when(direction == 0)
    def _():
        def body(i, h):
            return pair_scan(pl.multiple_of(i * BF16_SUBLANES, BF16_SUBLANES), h, False)
        carry[...] = lax.fori_loop(0, n_pairs, body, carry[...], unroll=2)

    @pl.when(direction == 1)
    def _():
        def body(i, h):
            return pair_scan(pl.multiple_of((n_pairs - 1 - i) * BF16_SUBLANES, BF16_SUBLANES), h, True)
        carry[...] = lax.fori_loop(0, n_pairs, body, carry[...], unroll=2)

    hfin_ref[...] = carry[...]


def _lru_scan(u, h0, conv_w, conv_b, wg_bf, bg, lam):
    b, s, d = u.shape
    tc = min(TC_SCAN, s)
    n = s // tc
    halo = BF16_SUBLANES
    per = tc // halo
    last_halo = s // halo - 1
    blk = d // N_LRU_BLOCKS

    def chunk_of(di, j):
        return j + di * (n - 1 - 2 * j)

    def u_chunk(di, j):
        return j + di * (n - 1 - j)

    return pl.pallas_call(
        _scan_kernel,
        grid=(b, 2, n),
        in_specs=[
            pl.BlockSpec((None, tc, d), lambda bi, di, j: (bi, u_chunk(di, j), 0)),
            pl.BlockSpec((None, halo, d),
                         lambda bi, di, j: (bi, jnp.maximum(u_chunk(di, j) * per - 1, 0), 0)),
            pl.BlockSpec((None, halo, d),
                         lambda bi, di, j: (bi, jnp.minimum((u_chunk(di, j) + 1) * per, last_halo), 0)),
            pl.BlockSpec((None, None, 1, d), lambda bi, di, j: (di, bi, 0, 0)),
            pl.BlockSpec((CONV_W, d), lambda bi, di, j: (0, 0)),
            pl.BlockSpec((1, d), lambda bi, di, j: (0, 0)),
            pl.BlockSpec((None, N_LRU_BLOCKS, blk, 2 * blk), lambda bi, di, j: (di, 0, 0, 0)),
            pl.BlockSpec((None, 2, d), lambda bi, di, j: (di, 0, 0)),
            pl.BlockSpec((None, 1, d), lambda bi, di, j: (di, 0, 0)),
        ],
        out_specs=[
            pl.BlockSpec((None, None, tc, d), lambda bi, di, j: (di, bi, chunk_of(di, j), 0)),
            pl.BlockSpec((None, None, 1, d), lambda bi, di, j: (di, bi, 0, 0)),
        ],
        out_shape=[jax.ShapeDtypeStruct((2, b, s, d), BF16), jax.ShapeDtypeStruct((2, b, 1, d), F32)],
        scratch_shapes=[
            pltpu.VMEM((tc + 2 * halo, d), F32),
            pltpu.VMEM((n, tc, d), F32),
            pltpu.VMEM((tc, d), F32),
            pltpu.VMEM((tc, d), F32),
            pltpu.VMEM((1, d), F32),
        ],
        compiler_params=_cparams(("arbitrary", "arbitrary", "arbitrary")),
        name="lru_scan",
    )(u, u, u, h0, conv_w, conv_b.reshape(1, d), wg_bf, bg, lam.reshape(2, 1, d))


def _lru_out_kernel(g_ref, hf_ref, hb_ref, x_ref, mod_ref, w_ref, o_ref):
    y = hf_ref[...].astype(F32) + hb_ref[...].astype(F32)
    t = (g_ref[...].astype(F32) * y).astype(BF16)
    o = jnp.dot(t, w_ref[...], preferred_element_type=F32)
    o_ref[...] = x_ref[...] + mod_ref[2:3, :] * o


def _lru_out(g, h, x, mod, w_bf):
    b, s, d = x.shape
    tm = min(TM_PROJ, s)
    row = pl.BlockSpec((None, tm, d), lambda bi, i: (bi, i, 0))
    return pl.pallas_call(
        _lru_out_kernel,
        grid=(b, s // tm),
        in_specs=[
            row,
            pl.BlockSpec((None, None, tm, d), lambda bi, i: (0, bi, i, 0)),
            pl.BlockSpec((None, None, tm, d), lambda bi, i: (1, bi, i, 0)),
            row,
            _mod_spec(mod, d),
            pl.BlockSpec((d, d), lambda bi, i: (0, 0)),
        ],
        out_specs=row,
        out_shape=jax.ShapeDtypeStruct((b, s, d), F32),
        compiler_params=_cparams(("parallel", "parallel")),
        name="lru_out",
    )(g, h, h, x, mod, w_bf)


def _attn_out_kernel(a_ref, x_ref, mod_ref, w_ref, o_ref):
    o = jnp.dot(a_ref[...], w_ref[...], preferred_element_type=F32)
    o_ref[...] = x_ref[...] + mod_ref[2:3, :] * o


def _attn_out(a, x, mod, w_bf):
    b, s, d = x.shape
    tm = min(TM_PROJ, s)
    row = pl.BlockSpec((None, tm, d), lambda bi, i: (bi, i, 0))
    return pl.pallas_call(
        _attn_out_kernel,
        grid=(b, s // tm),
        in_specs=[row, row, _mod_spec(mod, d), pl.BlockSpec((d, d), lambda bi, i: (0, 0))],
        out_specs=row,
        out_shape=jax.ShapeDtypeStruct((b, s, d), F32),
        compiler_params=_cparams(("parallel", "parallel")),
        name="attn_out",
    )(a, x, mod, w_bf)


def _first_hit(vals, target):
    hits = []
    found = None
    for v in vals:
        eq = v == target
        if found is None:
            hits.append(eq)
            found = eq
        else:
            hits.append(jnp.logical_and(eq, jnp.logical_not(found)))
            found = jnp.logical_or(found, eq)
    return hits


def _route_kernel(x_ref, mod_ref, nw_ref, rwt_ref, rb_ref, tri_ref, h_ref, pos_ref, col_ref, gcol_ref, stat_ref):
    n_exp = rwt_ref.shape[0] // 2
    tm = x_ref.shape[0]
    per_group = n_exp // N_EXPERT_GROUPS
    h = _modulate(x_ref[...], nw_ref[...], mod_ref[3:4, :], mod_ref[4:5, :])
    h_hi = h.astype(BF16)
    h_ref[...] = h_hi
    h_lo = (h - h_hi.astype(F32)).astype(BF16)
    nt = (((1,), (1,)), ((), ()))
    by_hi = lax.dot_general(rwt_ref[...], h_hi, nt, preferred_element_type=F32)
    by_lo = lax.dot_general(rwt_ref[0:n_exp, :], h_lo, nt, preferred_element_type=F32)
    logits = by_hi[:n_exp, :] + by_hi[n_exp:, :] + by_lo + rb_ref[...]
    rows = [logits[e:e + 1, :] for e in range(n_exp)]
    mx = functools.reduce(jnp.maximum, rows)
    ex = [jnp.exp(r - mx) for r in rows]
    z = functools.reduce(lambda p, q: p + q, ex)
    probs = [e_ / z for e_ in ex]
    scores = []
    for g in range(N_EXPERT_GROUPS):
        v0, v1, v2, v3 = probs[per_group * g:per_group * (g + 1)]
        hi01, lo01 = jnp.maximum(v0, v1), jnp.minimum(v0, v1)
        hi23, lo23 = jnp.maximum(v2, v3), jnp.minimum(v2, v3)
        top1 = jnp.maximum(hi01, hi23)
        top2 = jnp.maximum(jnp.minimum(hi01, hi23), jnp.maximum(lo01, lo23))
        scores.append(top1 + top2)
    best = functools.reduce(jnp.maximum, scores)
    sel = _first_hit(scores, best)
    masked = [jnp.where(sel[e // per_group], probs[e], -1.0) for e in range(n_exp)]
    m1 = functools.reduce(jnp.maximum, masked)
    is1 = _first_hit(masked, m1)
    masked2 = [jnp.where(is1[e], -2.0, masked[e]) for e in range(n_exp)]
    m2 = functools.reduce(jnp.maximum, masked2)
    is2 = _first_hit(masked2, m2)
    denom = m1 + m2
    gates = [jnp.where(is1[e], m1, jnp.where(is2[e], m2, 0.0)) / denom for e in range(n_exp)]
    zero = jnp.zeros_like(denom)
    chosen = [functools.reduce(lambda p, q: p + q,
                               [jnp.where(sel[g], gates[per_group * g + k], 0.0) for g in range(N_EXPERT_GROUPS)])
              for k in range(per_group)]
    chosen_hi = [v.astype(BF16).astype(F32) for v in chosen]
    chosen_lo = [v - hi for v, hi in zip(chosen, chosen_hi)]

    onehot = [jnp.where(sel[g], 1.0, 0.0) for g in range(N_EXPERT_GROUPS)]
    onehot_t = jnp.concatenate(onehot + [zero] * (SUBLANES - N_EXPERT_GROUPS), axis=0).astype(BF16)
    earlier = jnp.dot(onehot_t, tri_ref[...], preferred_element_type=F32)
    start = jnp.zeros((1, 1), F32)
    pos = zero
    starts, blocks = [], []
    for g in range(N_EXPERT_GROUPS):
        count = jnp.sum(onehot[g], axis=1, keepdims=True)
        n_blk = jnp.floor((count + (MOE_BLK - 1)) * (1.0 / MOE_BLK))
        pos = pos + onehot[g] * (start + earlier[g:g + 1, :])
        starts.append(start)
        blocks.append(n_blk)
        start = start + n_blk * MOE_BLK
    pos_ref[0:1, :] = pos
    pos_ref[1:SUBLANES, :] = jnp.zeros((SUBLANES - 1, tm), F32)
    assert ROUTE_POS_LANE == 2 * per_group
    per_token = jnp.concatenate(chosen_hi + chosen_lo + [pos, jnp.zeros((LANES - ROUTE_POS_LANE - 1, tm), F32)],
                                axis=0)
    per_token_t = per_token.T
    col_ref[...] = per_token_t
    gcol_ref[...] = per_token_t.astype(BF16)
    stat_ref[...] = jnp.concatenate([jnp.broadcast_to(v, (1, LANES)) for v in starts + blocks], axis=0)


def _route(x, mod, norm_w, router_w, router_b):
    b, s, d = x.shape
    n_exp = router_w.shape[1]
    assert n_exp // N_EXPERT_GROUPS == 4 and N_EXPERT_GROUPS == 4
    tm = min(TM_MOE, s)
    n_s = s // tm
    row = pl.BlockSpec((None, tm, d), lambda bi, i: (bi, i, 0))
    t_idx = np.arange(tm)
    tri = jnp.asarray(t_idx[:, None] < t_idx[None, :], dtype=BF16)
    rw_t = router_w.T
    rw_hi = rw_t.astype(BF16)
    rw_halves = jnp.concatenate([rw_hi, (rw_t - rw_hi.astype(F32)).astype(BF16)], axis=0)
    return pl.pallas_call(
        _route_kernel,
        grid=(b, n_s),
        in_specs=[
            row,
            _mod_spec(mod, d),
            pl.BlockSpec((1, d), lambda bi, i: (0, 0)),
            pl.BlockSpec((2 * n_exp, d), lambda bi, i: (0, 0)),
            pl.BlockSpec((n_exp, 1), lambda bi, i: (0, 0)),
            pl.BlockSpec((tm, tm), lambda bi, i: (0, 0)),
        ],
        out_specs=[row,
                   pl.BlockSpec((SUBLANES, tm), lambda bi, i: (0, bi * n_s + i)),
                   pl.BlockSpec((tm, LANES), lambda bi, i: (bi * n_s + i, 0)),
                   pl.BlockSpec((tm, LANES), lambda bi, i: (bi * n_s + i, 0)),
                   pl.BlockSpec((None, SUBLANES, LANES), lambda bi, i: (bi * n_s + i, 0, 0))],
        out_shape=[jax.ShapeDtypeStruct((b, s, d), BF16),
                   jax.ShapeDtypeStruct((SUBLANES, b * s), F32),
                   jax.ShapeDtypeStruct((b * s, LANES), F32),
                   jax.ShapeDtypeStruct((b * s, LANES), BF16),
                   jax.ShapeDtypeStruct((b * n_s, SUBLANES, LANES), F32)],
        compiler_params=_cparams(("parallel", "parallel")),
        name="moe_route",
    )(x, mod, norm_w.reshape(1, d), rw_halves, router_b.reshape(n_exp, 1), tri)


def _moe_kernel(stat_ref, h_ref, mrow_ref, mcol_ref, gcol_ref, x_ref, mod_ref, w1_ref, w3_ref, w2_ref,
                o_ref, xc, gc, yc):
    pair = pl.program_id(0)
    step = pl.program_id(1)
    tm = h_ref.shape[0]
    rows = xc.shape[1]
    per_group = N_EXPERT_GROUPS
    n_gather, n_expert = 2, 2 * N_EXPERT_GROUPS

    for slot in range(2):
        @pl.when(step == slot)
        def _(slot=slot):
            pos_row = mrow_ref[0:1, :].astype(jnp.int32)
            h = h_ref[...]
            gcol = gcol_ref[...]
            for c in range(rows // MOE_CHUNK):
                r_id = lax.broadcasted_iota(jnp.int32, (MOE_CHUNK, tm), 0) + c * MOE_CHUNK
                perm = jnp.where(pos_row == r_id, 1.0, 0.0).astype(BF16)
                sl = slice(c * MOE_CHUNK, (c + 1) * MOE_CHUNK)
                xc[slot, sl, :] = jnp.dot(perm, h, preferred_element_type=F32).astype(BF16)
                gc[slot, sl, :] = jnp.dot(perm, gcol, preferred_element_type=F32)
            yc[slot] = jnp.zeros((rows, yc.shape[2]), BF16)

    @pl.when(jnp.logical_and(step >= n_gather, step < n_gather + n_expert))
    def _():
        e_step = step - n_gather
        group = e_step // 2
        slot = ((e_step + 1) // 2) % 2
        base = (2 * pair + slot) * SUBLANES
        start = stat_ref[base + group]
        n_blk = stat_ref[base + per_group + group]

        def experts(r0, n_rows):
            xb = xc[slot, pl.ds(r0, n_rows), :]
            g = gc[slot, pl.ds(r0, n_rows), :]
            parts = []
            for k in range(per_group):
                a = jnp.dot(xb, w1_ref[k], preferred_element_type=F32)
                c = jnp.dot(xb, w3_ref[k], preferred_element_type=F32)
                gate = g[:, k:k + 1] + g[:, per_group + k:per_group + k + 1]
                parts.append(((a * _sigmoid(a)) * c * gate).astype(BF16))
            t = jnp.concatenate(parts, axis=1)
            yc[slot, pl.ds(r0, n_rows), :] = jnp.dot(t, w2_ref[...], preferred_element_type=F32).astype(BF16)

        def block_pair(j, carry):
            experts(pl.multiple_of(start + j * (2 * MOE_BLK), MOE_BLK), 2 * MOE_BLK)
            return carry

        lax.fori_loop(0, n_blk // 2, block_pair, 0)

        @pl.when(n_blk % 2 == 1)
        def _():
            experts(pl.multiple_of(start + (n_blk - 1) * MOE_BLK, MOE_BLK), MOE_BLK)

    for slot in range(2):
        @pl.when(step == n_gather + n_expert + slot)
        def _(slot=slot):
            y_sorted = yc[slot]
            for c in range(tm // MOE_CHUNK):
                sl = slice(c * MOE_CHUNK, (c + 1) * MOE_CHUNK)
                pos_col = mcol_ref[sl, ROUTE_POS_LANE:ROUTE_POS_LANE + 1].astype(jnp.int32)
                c_id = lax.broadcasted_iota(jnp.int32, (MOE_CHUNK, rows), 1)
                perm_t = jnp.where(pos_col == c_id, 1.0, 0.0).astype(BF16)
                y = jnp.dot(perm_t, y_sorted, preferred_element_type=F32)
                o_ref[sl, :] = x_ref[sl, :] + mod_ref[5:6, :] * y


def _moe(h, pos_rows, col, gcol, stat, x, mod, w1_bf, w3_bf, w2g):
    b, s, d = x.shape
    n_tok = b * s
    tm = min(TM_MOE, s)
    n_tiles = n_tok // tm
    assert n_tiles % 2 == 0
    rows = tm + N_EXPERT_GROUPS * MOE_BLK
    rows = -(-rows // MOE_CHUNK) * MOE_CHUNK
    per_group, ff = w1_bf.shape[0] // N_EXPERT_GROUPS, w1_bf.shape[2]
    n_gather, n_expert = 2, 2 * N_EXPERT_GROUPS
    n_steps = n_gather + n_expert + 2

    def gather_tile(p, t):
        return 2 * p + jnp.minimum(t, 1)

    def scatter_tile(p, t):
        return 2 * p + jnp.where(t == n_steps - 1, 1, 0)

    def group_of(t):
        return jnp.clip((t - n_gather) // 2, 0, N_EXPERT_GROUPS - 1)

    if mod.shape[0] == 1:
        mod_spec = pl.BlockSpec((None, N_MOD, d), lambda p, t, st: (0, 0, 0))
    else:
        mod_spec = pl.BlockSpec((None, N_MOD, d), lambda p, t, st: ((scatter_tile(p, t) * tm) // s, 0, 0))
    stat_i = stat[:, :, 0].astype(jnp.int32).reshape(-1)
    once = pl.Buffered(1)

    grid_spec = pltpu.PrefetchScalarGridSpec(
        num_scalar_prefetch=1,
        grid=(n_tiles // 2, n_steps),
        in_specs=[
            pl.BlockSpec((tm, d), lambda p, t, st: (gather_tile(p, t), 0), pipeline_mode=once),
            pl.BlockSpec((SUBLANES, tm), lambda p, t, st: (0, gather_tile(p, t))),
            pl.BlockSpec((tm, LANES), lambda p, t, st: (scatter_tile(p, t), 0)),
            pl.BlockSpec((tm, LANES), lambda p, t, st: (gather_tile(p, t), 0)),
            pl.BlockSpec((tm, d), lambda p, t, st: (scatter_tile(p, t), 0), pipeline_mode=once),
            mod_spec,
            pl.BlockSpec((per_group, d, ff), lambda p, t, st: (group_of(t), 0, 0)),
            pl.BlockSpec((per_group, d, ff), lambda p, t, st: (group_of(t), 0, 0)),
            pl.BlockSpec((None, per_group * ff, d), lambda p, t, st: (group_of(t), 0, 0)),
        ],
        out_specs=pl.BlockSpec((tm, d), lambda p, t, st: (scatter_tile(p, t), 0), pipeline_mode=once),
        scratch_shapes=[pltpu.VMEM((2, rows, d), BF16), pltpu.VMEM((2, rows, LANES), F32),
                        pltpu.VMEM((2, rows, d), BF16)],
    )
    out = pl.pallas_call(
        _moe_kernel,
        grid_spec=grid_spec,
        out_shape=jax.ShapeDtypeStruct((n_tok, d), F32),
        compiler_params=_cparams(("parallel", "arbitrary"), VMEM_LIMIT_MOE),
        name="moe_experts",
    )(stat_i, h.reshape(n_tok, d), pos_rows, col, gcol, x.reshape(n_tok, d), mod, w1_bf, w3_bf, w2g)
    return out.reshape(b, s, d)


def _group_expert_weights(w1, w3, w2):
    d = w2.shape[2]
    return w1.astype(BF16), w3.astype(BF16), w2.astype(BF16).reshape(N_EXPERT_GROUPS, -1, d)


def _moe_block(x, mod, norm_w, router_w, router_b, w1_bf, w3_bf, w2g):
    shape = x.shape
    if mod.shape[0] == 1:
        x = x.reshape(1, -1, shape[-1])
    h, pos_rows, col, gcol, stat = _route(x, mod, norm_w, router_w, router_b)
    return _moe(h, pos_rows, col, gcol, stat, x, mod, w1_bf, w3_bf, w2g).reshape(shape)


def _head_norm(t, w_row, ones_blockdiag):
    ss = jnp.dot((t * t).astype(BF16), ones_blockdiag, preferred_element_type=F32)
    return t * lax.rsqrt(ss * (1.0 / HEAD_DIM) + EPS) * w_row


def _rope(t, cos, sin_next, sin_prev):
    quarter = HEAD_DIM // 4
    outs = []
    for nb in range(t.shape[1] // LANES):
        tb = t[:, LANES * nb:LANES * (nb + 1)]
        outs.append(tb * cos
                    + pltpu.roll(tb, LANES - quarter, 1) * sin_next
                    + pltpu.roll(tb, quarter, 1) * sin_prev)
    return jnp.concatenate(outs, axis=1)


def _store_kv(k, v, k_ref, vt_ref):
    tm = k.shape[0]
    vt = v.T
    pad = jnp.concatenate([jnp.ones((1, tm), F32), jnp.zeros((BF16_SUBLANES - 1, tm), F32)], axis=0)
    for h in range(N_KV_HEADS):
        k_ref[h] = k[:, HEAD_DIM * h:HEAD_DIM * (h + 1)].astype(BF16)
        vt_ref[h] = jnp.concatenate([vt[HEAD_DIM * h:HEAD_DIM * (h + 1), :], pad], axis=0).astype(BF16)


def _qkv_kernel(x_ref, mod_ref, nw_ref, w_ref, qn_ref, kn_ref, ones_ref, cos_ref, sn_ref, sp_ref,
                k_all_ref, vt_all_ref, q_ref, k_ref, vt_ref):
    del k_all_ref, vt_all_ref
    n_q = q_ref.shape[-1]
    n_kv = N_KV_HEADS * HEAD_DIM
    h = _modulate(x_ref[...], nw_ref[...], mod_ref[0:1, :], mod_ref[1:2, :])
    r = jnp.dot(h.astype(BF16), w_ref[...], preferred_element_type=F32)
    cos, sn, sp = cos_ref[...], sn_ref[...], sp_ref[...]
    q = _head_norm(r[:, :n_q], qn_ref[...], ones_ref[...])
    q_ref[...] = (_rope(q, cos, sn, sp) * (math.log2(math.e) / math.sqrt(HEAD_DIM))).astype(BF16)
    k = _head_norm(r[:, n_q:n_q + n_kv], kn_ref[...], ones_ref[0:n_kv, 0:n_kv])
    _store_kv(_rope(k, cos, sn, sp), r[:, n_q + n_kv:], k_ref, vt_ref)


def _kv_ctx_kernel(x_ref, mod_ref, nw_ref, w_ref, kn_ref, ones_ref, k_ref, vt_ref):
    n_kv = N_KV_HEADS * HEAD_DIM
    h = _modulate(x_ref[...], nw_ref[...], mod_ref[0:1, :], mod_ref[1:2, :])
    r = jnp.dot(h.astype(BF16), w_ref[...], preferred_element_type=F32)
    _store_kv(_head_norm(r[:, :n_kv], kn_ref[...], ones_ref[...]), r[:, n_kv:], k_ref, vt_ref)


def _rope_tables(s):
    half = HEAD_DIM // 2
    quarter = half // 2
    t = np.arange(s)
    pos_r = (t // GRID_W).astype(np.float32)[:, None]
    pos_c = (t % GRID_W).astype(np.float32)[:, None]
    lane = np.arange(LANES)[None, :]
    in_head = lane % HEAD_DIM
    in_half = in_head % half
    freq = (in_half % quarter).astype(np.float32)
    inv = np.float32(ROPE_THETA) ** (-freq / np.float32(quarter))
    ang = (np.where(in_head < half, pos_r, pos_c) * inv).astype(np.float32)
    cos = np.cos(ang).astype(np.float32)
    sin = np.sin(ang).astype(np.float32)
    first = in_half < quarter
    zero = np.float32(0.0)
    return (jnp.asarray(cos), jnp.asarray(np.where(first, -sin, zero)), jnp.asarray(np.where(first, zero, sin)))


def _ones_blockdiag(n):
    i = np.arange(n)
    return jnp.asarray(i[:, None] // HEAD_DIM == i[None, :] // HEAD_DIM, dtype=BF16)


def _qkv(x, mod, norm_w, w_bf, qn_w, kn_w, k_all, vt_all, n_ctx):
    b, s, d = x.shape
    n_kv = N_KV_HEADS * HEAD_DIM
    n_q = w_bf.shape[1] - 2 * n_kv
    tm = n_ctx
    assert s % tm == 0 and tm % LANES == 0
    cos, sn, sp = _rope_tables(s)
    tab = pl.BlockSpec((tm, LANES), lambda i, bi: (i, 0))

    def row(w):
        return pl.BlockSpec((None, tm, w), lambda i, bi: (bi, i, 0))

    def const(shape):
        return pl.BlockSpec(shape, lambda i, bi: (0,) * len(shape))

    return pl.pallas_call(
        _qkv_kernel,
        grid=(s // tm, b),
        in_specs=[
            row(d),
            pl.BlockSpec((None, N_MOD, d), lambda i, bi: (bi, 0, 0)),
            const((1, d)),
            const((d, n_q + 2 * n_kv)),
            const((1, n_q)),
            const((1, n_kv)),
            const((n_q, n_q)),
            tab, tab, tab,
            pl.BlockSpec(memory_space=pl.ANY),
            pl.BlockSpec(memory_space=pl.ANY),
        ],
        out_specs=[row(n_q),
                   pl.BlockSpec((None, N_KV_HEADS, tm, HEAD_DIM), lambda i, bi: (bi, 0, i + 1, 0)),
                   pl.BlockSpec((None, N_KV_HEADS, vt_all.shape[2], tm), lambda i, bi: (bi, 0, 0, i + 1))],
        out_shape=[jax.ShapeDtypeStruct((b, s, n_q), BF16),
                   jax.ShapeDtypeStruct(k_all.shape, BF16),
                   jax.ShapeDtypeStruct(vt_all.shape, BF16)],
        input_output_aliases={10: 1, 11: 2},
        compiler_params=_cparams(("parallel", "parallel")),
        name="attn_qkv",
    )(x, mod, norm_w.reshape(1, d), w_bf,
      jnp.tile(qn_w, n_q // HEAD_DIM).reshape(1, n_q), jnp.tile(kn_w, N_KV_HEADS).reshape(1, n_kv),
      _ones_blockdiag(n_q), cos, sn, sp, k_all, vt_all)


def _kv_ctx(ctx, mod, norm_w, w_kv_bf, kn_w, n_keys):
    b, s, d = ctx.shape
    n_kv = N_KV_HEADS * HEAD_DIM
    v_rows = HEAD_DIM + BF16_SUBLANES

    def const(shape):
        return pl.BlockSpec(shape, lambda bi: (0,) * len(shape))

    return pl.pallas_call(
        _kv_ctx_kernel,
        grid=(b,),
        in_specs=[pl.BlockSpec((None, s, d), lambda bi: (bi, 0, 0)),
                  pl.BlockSpec((None, N_MOD, d), lambda bi: (0, 0, 0)),
                  const((1, d)), const((d, 2 * n_kv)), const((1, n_kv)), const((n_kv, n_kv))],
        out_specs=[pl.BlockSpec((None, N_KV_HEADS, s, HEAD_DIM), lambda bi: (bi, 0, 0, 0)),
                   pl.BlockSpec((None, N_KV_HEADS, v_rows, s), lambda bi: (bi, 0, 0, 0))],
        out_shape=[jax.ShapeDtypeStruct((b, N_KV_HEADS, n_keys, HEAD_DIM), BF16),
                   jax.ShapeDtypeStruct((b, N_KV_HEADS, v_rows, n_keys), BF16)],
        compiler_params=_cparams(("parallel",)),
        name="attn_kv_ctx",
    )(ctx, mod, norm_w.reshape(1, d), w_kv_bf, jnp.tile(kn_w, N_KV_HEADS).reshape(1, n_kv),
      _ones_blockdiag(n_kv))


def _attn_kernel(q_ref, k_ref, vt_ref, o_ref, s_scr):
    q = q_ref[...]
    k = k_ref[...]
    vt = vt_ref[...]
    n_keys = k.shape[0]
    n_head_pairs = q.shape[1] // (2 * HEAD_DIM)
    n_units = n_head_pairs * (q.shape[0] // ATTN_QSUB)

    def scores(u):
        rows = slice(ATTN_QSUB * (u // n_head_pairs), ATTN_QSUB * (u // n_head_pairs + 1))
        h0 = 2 * (u % n_head_pairs)
        q2 = jnp.concatenate([q[rows, HEAD_DIM * (h0 + i):HEAD_DIM * (h0 + i + 1)] for i in range(2)], axis=0)
        st = lax.dot_general(k, q2, (((1,), (1,)), ((), ())), preferred_element_type=F32)
        s_scr[u % 2] = st
        slabs = st.reshape(n_keys // ATTN_SLAB, ATTN_SLAB, 2 * ATTN_QSUB)
        return jnp.max(jnp.max(slabs, axis=0), axis=0, keepdims=True)

    def weighted_values(u, m):
        p = jnp.exp2(s_scr[u % 2] - m).astype(BF16)
        ot = jnp.dot(vt, p, preferred_element_type=F32)
        on = ot[:HEAD_DIM, :] / ot[HEAD_DIM:HEAD_DIM + 1, :]
        return [on[:, :ATTN_QSUB], on[:, ATTN_QSUB:]]

    outs = []
    m_next = scores(0)
    for u in range(n_units):
        m = m_next
        if u + 1 < n_units:
            m_next = scores(u + 1)
        outs += weighted_values(u, m)
        if (u + 1) % n_head_pairs == 0:
            rows = slice(ATTN_QSUB * (u // n_head_pairs), ATTN_QSUB * (u // n_head_pairs + 1))
            o_ref[rows, :] = jnp.concatenate(outs, axis=0).T.astype(BF16)
            outs = []


def _attention(q, k, vt):
    assert k.shape[2] % ATTN_SLAB == 0
    b, s, n_q = q.shape
    n_keys = k.shape[2]
    group_w = n_q // N_KV_HEADS
    tq = min(TQ_ATTN, s)
    return pl.pallas_call(
        _attn_kernel,
        grid=(b, N_KV_HEADS, s // tq),
        in_specs=[
            pl.BlockSpec((None, tq, group_w), lambda bi, g, i: (bi, i, g)),
            pl.BlockSpec((None, None, n_keys, HEAD_DIM), lambda bi, g, i: (bi, g, 0, 0)),
            pl.BlockSpec((None, None, vt.shape[2], n_keys), lambda bi, g, i: (bi, g, 0, 0)),
        ],
        out_specs=pl.BlockSpec((None, tq, group_w), lambda bi, g, i: (bi, i, g)),
        out_shape=jax.ShapeDtypeStruct((b, s, n_q), BF16),
        scratch_shapes=[pltpu.VMEM((2, n_keys, 2 * ATTN_QSUB), F32)],
        compiler_params=_cparams(("parallel", "parallel", "parallel")),
        name="attn_core",
    )(q, k, vt)


def _modulations(c, c_ctx, ada_w, ada_b):
    b, d = c.shape
    depth = ada_w.shape[0]
    assert b <= SUBLANES
    cond = jnp.zeros((2 * SUBLANES, d), F32).at[:b].set(c).at[b].set(c_ctx)
    mods = _ada(cond, ada_w, ada_b)
    mod_x = [mods[l, :b].reshape(b, N_MOD, d) for l in range(depth)]
    mod_c = [mods[l, b:b + 1].reshape(1, N_MOD, d) for l in range(depth)]
    return mod_x, mod_c


def _lru_layer(x, ctx, mod_x, mod_c, norm_w, in_w, conv_w, conv_b, ga_w, ga_b, gx_w, gx_b, lam, out_w):
    b, _, d = x.shape
    in_bf = in_w.astype(BF16)
    out_bf = out_w.astype(BF16)
    wg = jnp.concatenate([ga_w, gx_w], axis=-1).astype(BF16)
    bg = jnp.stack([ga_b, gx_b], axis=1)
    g_c, u_c = _inproj(ctx, mod_c, norm_w, in_bf)
    g_x, u_x = _inproj(x, mod_x, norm_w, in_bf)
    h_c, h_c_end = _lru_scan(u_c, jnp.zeros((2, b, 1, d), F32), conv_w, conv_b, wg, bg, lam)
    h_x, _ = _lru_scan(u_x, h_c_end, conv_w, conv_b, wg, bg, lam)
    return _lru_out(g_x, h_x, x, mod_x, out_bf), _lru_out(g_c, h_c, ctx, mod_c, out_bf)


def _attn_layer(x, ctx, mod_x, mod_c, norm_w, qkv_w, qn_w, kn_w, o_w):
    b, s, _ = x.shape
    n_kv = N_KV_HEADS * HEAD_DIM
    qkv_bf = qkv_w.astype(BF16)
    n_q = qkv_bf.shape[1] - 2 * n_kv
    n_ctx = ctx.shape[1]
    k_all, vt_all = _kv_ctx(ctx, mod_c, norm_w, qkv_bf[:, n_q:], kn_w, n_ctx + s)
    q, k_all, vt_all = _qkv(x, mod_x, norm_w, qkv_bf, qn_w, kn_w, k_all, vt_all, n_ctx)
    att = _attention(q, k_all, vt_all)
    return _attn_out(att, x, mod_x, o_w.astype(BF16))


def kernel(x, c, ctx, c_ctx, ada_w, ada_b, norm_mix_w, norm_ffn_w, lru_in_w, lru_conv_w, lru_conv_b, lru_gate_a_w, lru_gate_a_b, lru_gate_x_w, lru_gate_x_b, lru_lambda, lru_out_w, attn_qkv_w, attn_q_norm_w, attn_k_norm_w, attn_o_w, router_w, router_b, moe_w1, moe_w3, moe_w2):
    assert ada_w.shape[0] == 2
    mod_x, mod_c = _modulations(c, c_ctx, ada_w, ada_b)

    x, ctx = _lru_layer(x, ctx, mod_x[0], mod_c[0], norm_mix_w[0], lru_in_w[0], lru_conv_w[0], lru_conv_b[0],
                        lru_gate_a_w[0], lru_gate_a_b[0], lru_gate_x_w[0], lru_gate_x_b[0], lru_lambda[0],
                        lru_out_w[0])
    w1, w3, w2 = _group_expert_weights(moe_w1[0], moe_w3[0], moe_w2[0])
    ctx = _moe_block(ctx, mod_c[0], norm_ffn_w[0], router_w, router_b, w1, w3, w2)
    x = _moe_block(x, mod_x[0], norm_ffn_w[0], router_w, router_b, w1, w3, w2)

    x = _attn_layer(x, ctx, mod_x[1], mod_c[1], norm_mix_w[1], attn_qkv_w[0], attn_q_norm_w[0],
                    attn_k_norm_w[0], attn_o_w[0])
    w1, w3, w2 = _group_expert_weights(moe_w1[1], moe_w3[1], moe_w2[1])
    return _moe_block(x, mod_x[1], norm_ffn_w[1], router_w, router_b, w1, w3, w2)
```
